```python
import jax, jax.numpy as jnp
from jax import lax
import numpy as np

D_MODEL = 4096
BATCH = 2
SEQ = 4096
DEPTH = 4

MEM_LEN = 256
EPS = 1e-6
D_A = D_MODEL // 2
D_B = D_MODEL // 2
A_HEADS = 8
A_HEAD_DIM = D_A // A_HEADS
CHUNK = 128
B_GROUPS = 8
B_GROUP_DIM = D_B // B_GROUPS
CONV_WIDTH = 31
C_GROUPS = 4
C_GROUP_DIM = D_MODEL // C_GROUPS
POOL_WINDOWS = (2, 4, 8, 16)
X_HEADS = 4
X_HEAD_DIM = 512
X_DIM = X_HEADS * X_HEAD_DIM
D_FF = -(-8 * D_MODEL // (3 * 256)) * 256
N_EVEN = (DEPTH + 1) // 2
N_ODD = DEPTH // 2

kernel_name = "hybrid_sgu_conformer_pool_memxattn"


def rms_norm(x, g):
    xf = x.astype(jnp.float32)
    y = xf * lax.rsqrt(jnp.mean(xf * xf, axis=-1, keepdims=True) + EPS)
    return (y * g.astype(jnp.float32)).astype(x.dtype)


def layer_norm(x, g):
    xf = x.astype(jnp.float32)
    mu = jnp.mean(xf, axis=-1, keepdims=True)
    xc = xf - mu
    var = jnp.mean(xc * xc, axis=-1, keepdims=True)
    return (xc * lax.rsqrt(var + EPS) * g.astype(jnp.float32)).astype(x.dtype)


def spatial_gating(u, v, w_s, b_s, g_v):
    b, s = u.shape[:2]
    n_chunks = s // CHUNK
    v = layer_norm(v, g_v).reshape(b, n_chunks, CHUNK, A_HEADS, A_HEAD_DIM)
    mask = jnp.tril(jnp.ones((CHUNK, CHUNK), dtype=bool))
    w = jnp.where(mask[None], w_s, 0.0).astype(v.dtype)
    vm = jnp.einsum('hts,bnshd->bnthd', w, v) + b_s.T.astype(v.dtype)[None, None, :, :, None]
    return u * vm.reshape(b, s, A_HEADS, A_HEAD_DIM)


def conformer_conv(h, conv_w, conv_b, gn_g):
    b, s, _ = h.shape
    a, gate = jnp.split(h, 2, axis=-1)
    z = a * jax.nn.sigmoid(gate)
    z = lax.conv_general_dilated(
        z, conv_w[:, None, :].astype(z.dtype), window_strides=(1,),
        padding=((CONV_WIDTH - 1, 0),), dimension_numbers=('NWC', 'WIO', 'NWC'),
        feature_group_count=D_B) + conv_b.astype(z.dtype)
    z = layer_norm(z.reshape(b, s, B_GROUPS, B_GROUP_DIM), gn_g).reshape(b, s, D_B)
    return jax.nn.silu(z)


def multiscale_pool(h):
    b, s, _ = h.shape
    hf = h.astype(jnp.float32).reshape(b, s, C_GROUPS, C_GROUP_DIM)
    cs = jnp.cumsum(hf, axis=1)
    cs = jnp.concatenate([jnp.zeros_like(cs[:, :1]), cs], axis=1)
    t = jnp.arange(s)[:, None]
    win = jnp.array(POOL_WINDOWS, dtype=jnp.int32)[None, :]
    start = jnp.maximum(t + 1 - win, 0)
    g_idx = jnp.arange(C_GROUPS)[None, :]
    win_sum = cs[:, 1:] - cs[:, start, g_idx]
    count = (t + 1 - start).astype(jnp.float32)[None, :, :, None]
    return (win_sum / count - hf).astype(h.dtype)


def ab_mixer(h, w_in, w_s, b_s, g_v, conv_w, conv_b, gn_g, w_out):
    b, s, _ = h.shape
    z = h @ w_in
    za, zb = z[..., :2 * D_A], z[..., 2 * D_A:]
    za = jax.nn.gelu(za)
    u = za[..., :D_A].reshape(b, s, A_HEADS, A_HEAD_DIM)
    v = za[..., D_A:].reshape(b, s, A_HEADS, A_HEAD_DIM)
    ya = spatial_gating(u, v, w_s, b_s, g_v).reshape(b, s, D_A)
    yb = conformer_conv(zb, conv_w, conv_b, gn_g)
    return jnp.concatenate([ya, yb], axis=-1) @ w_out


def c_mixer(h, w_in, group_w, scale, w_out):
    b, s, _ = h.shape
    p = multiscale_pool(h @ w_in)
    y = jnp.einsum('bsgc,gcd->bsgd', p, group_w).reshape(b, s, D_MODEL)
    return (y * scale) @ w_out


def cross_attention(h, m, wq, wk, wv, wo):
    b, s, _ = h.shape
    n_mem = m.shape[1]
    q = (h @ wq).reshape(b, s, X_HEADS, X_HEAD_DIM)
    k = (m @ wk).reshape(b, n_mem, X_HEADS, X_HEAD_DIM)
    v = (m @ wv).reshape(b, n_mem, X_HEADS, X_HEAD_DIM)
    scores = jnp.einsum('bshd,bmhd->bhsm', q, k).astype(jnp.float32) * (X_HEAD_DIM ** -0.5)
    p = jax.nn.softmax(scores, axis=-1).astype(v.dtype)
    o = jnp.einsum('bhsm,bmhd->bshd', p, v).reshape(b, s, X_DIM)
    return o @ wo


def swiglu(h, w_gate, w_up, w_down):
    return (jax.nn.silu(h @ w_gate) * (h @ w_up)) @ w_down


def setup_inputs(seed: int = 0) -> dict:
    key = jax.random.key(seed)
    ks = iter(jax.random.split(key, 32))

    def nrm(shape, scale):
        return jax.random.normal(next(ks), shape, jnp.float32) * scale

    def gain(shape):
        return 1.0 + 0.05 * jax.random.normal(next(ks), shape, jnp.float32)

    return {
        "x": nrm((BATCH, SEQ, D_MODEL), 1.0),
        "mem": nrm((BATCH, MEM_LEN, D_MODEL), 1.0),
        "mix_pre_g": gain((DEPTH, D_MODEL)),
        "mix_post_g": gain((DEPTH, D_MODEL)),
        "xattn_pre_g": gain((DEPTH, D_MODEL)),
        "mem_norm_g": gain((DEPTH, D_MODEL)),
        "xattn_post_g": gain((DEPTH, D_MODEL)),
        "ffn_pre_g": gain((DEPTH, D_MODEL)),
        "ffn_post_g": gain((DEPTH, D_MODEL)),
        "ab_w_in": nrm((N_EVEN, D_MODEL, 2 * D_A + 2 * D_B), D_MODEL ** -0.5),
        "a_w_s": nrm((N_EVEN, A_HEADS, CHUNK, CHUNK), CHUNK ** -0.5),
        "a_b_s": 1.0 + 0.1 * jax.random.normal(next(ks), (N_EVEN, A_HEADS, CHUNK), jnp.float32),
        "a_v_norm_g": gain((N_EVEN, A_HEADS, A_HEAD_DIM)),
        "b_conv_w": nrm((N_EVEN, CONV_WIDTH, D_B), CONV_WIDTH ** -0.5),
        "b_conv_b": nrm((N_EVEN, D_B), 0.01),
        "b_gn_g": gain((N_EVEN, B_GROUPS, B_GROUP_DIM)),
        "ab_w_out": nrm((N_EVEN, D_A + D_B, D_MODEL), (D_A + D_B) ** -0.5),
        "c_w_in": nrm((N_ODD, D_MODEL, D_MODEL), D_MODEL ** -0.5),
        "c_group_w": nrm((N_ODD, C_GROUPS, C_GROUP_DIM, C_GROUP_DIM), C_GROUP_DIM ** -0.5),
        "c_scale": gain((N_ODD, D_MODEL)),
        "c_w_out": nrm((N_ODD, D_MODEL, D_MODEL), D_MODEL ** -0.5),
        "xq_w": nrm((DEPTH, D_MODEL, X_DIM), D_MODEL ** -0.5),
        "xk_w": nrm((DEPTH, D_MODEL, X_DIM), D_MODEL ** -0.5),
        "xv_w": nrm((DEPTH, D_MODEL, X_DIM), D_MODEL ** -0.5),
        "xo_w": nrm((DEPTH, X_DIM, D_MODEL), X_DIM ** -0.5),
        "ffn_gate_w": nrm((DEPTH, D_MODEL, D_FF), D_MODEL ** -0.5),
        "ffn_up_w": nrm((DEPTH, D_MODEL, D_FF), D_MODEL ** -0.5),
        "ffn_down_w": nrm((DEPTH, D_FF, D_MODEL), D_FF ** -0.5),
    }


def reference(x, mem, mix_pre_g, mix_post_g, xattn_pre_g, mem_norm_g, xattn_post_g,
              ffn_pre_g, ffn_post_g, ab_w_in, a_w_s, a_b_s, a_v_norm_g, b_conv_w,
              b_conv_b, b_gn_g, ab_w_out, c_w_in, c_group_w, c_scale, c_w_out,
              xq_w, xk_w, xv_w, xo_w, ffn_gate_w, ffn_up_w, ffn_down_w):
    for i in range(DEPTH):
        j = i // 2
        h = rms_norm(x, mix_pre_g[i])
        if i % 2 == 0:
            y = ab_mixer(h, ab_w_in[j], a_w_s[j], a_b_s[j], a_v_norm_g[j],
                         b_conv_w[j], b_conv_b[j], b_gn_g[j], ab_w_out[j])
        else:
            y = c_mixer(h, c_w_in[j], c_group_w[j], c_scale[j], c_w_out[j])
        x = x + rms_norm(y, mix_post_g[i])
        h = rms_norm(x, xattn_pre_g[i])
        m = rms_norm(mem, mem_norm_g[i])
        y = cross_attention(h, m, xq_w[i], xk_w[i], xv_w[i], xo_w[i])
        x = x + rms_norm(y, xattn_post_g[i])
        h = rms_norm(x, ffn_pre_g[i])
        y = swiglu(h, ffn_gate_w[i], ffn_up_w[i], ffn_down_w[i])
        x = x + rms_norm(y, ffn_post_g[i])
    return x
```

```python
import functools

import jax
import jax.numpy as jnp
from jax import lax
from jax.experimental import pallas as pl
from jax.experimental.pallas import tpu as pltpu

D_MODEL = 4096
BATCH = 2
SEQ = 4096
DEPTH = 4
TOKENS = BATCH * SEQ
MEM_LEN = 256
EPS = 1e-6
D_A = D_MODEL // 2
D_B = D_MODEL // 2
A_HEADS = 8
A_HEAD_DIM = D_A // A_HEADS
CHUNK = 128
B_GROUPS = 8
B_GROUP_DIM = D_B // B_GROUPS
CONV_WIDTH = 31
C_GROUPS = 4
C_GROUP_DIM = D_MODEL // C_GROUPS
POOL_WINDOWS = (2, 4, 8, 16)
X_HEADS = 4
X_HEAD_DIM = 512
X_DIM = X_HEADS * X_HEAD_DIM
D_FF = -(-8 * D_MODEL // (3 * 256)) * 256

F32 = jnp.float32
BF16 = jnp.bfloat16

V7X_VMEM_BYTES = 64 * 1024 * 1024
V7X_LANES = 128
V7X_SUBLANES = 8

MM_BM = 1024
MM_BN = 1024
FF_BN = 512
FF_PAD = 1024
D_FF_PAD = -(-D_FF // FF_PAD) * FF_PAD
FF_BK = D_FF_PAD // 4
ROW_BM = 256
MIX_BM = 512
CONV_HALO = 32
POOL_HALO = 16
ROW_CHUNK = 64
ATT_BQ = 512


def _vmem_limit(block_bytes, scratch_bytes=0):
    need = 3 * block_bytes + scratch_bytes
    return int(min(max(need, 16 * 1024 * 1024), V7X_VMEM_BYTES - 4 * 1024 * 1024))


def _nbytes(shape, dtype):
    n = 1
    for s in shape:
        n *= s
    return n * jnp.dtype(dtype).itemsize


def _cparams(n_axes, block_bytes, scratch_bytes=0, reduction_last=False):
    sem = ["parallel"] * n_axes
    if reduction_last:
        sem[-1] = "arbitrary"
    return pltpu.CompilerParams(
        dimension_semantics=tuple(sem),
        vmem_limit_bytes=_vmem_limit(block_bytes, scratch_bytes),
    )


def _sigmoid(x):
    return 1.0 / (1.0 + jnp.exp(-x))


def _silu(x):
    return x * _sigmoid(x)


def _rms(x, g):
    return x * lax.rsqrt(jnp.mean(x * x, axis=-1, keepdims=True) + EPS) * g


def _layer_norm(x, g):
    mu = jnp.mean(x, axis=-1, keepdims=True)
    xc = x - mu
    var = jnp.mean(xc * xc, axis=-1, keepdims=True)
    return xc * lax.rsqrt(var + EPS) * g


def _rmsnorm_kernel(x_ref, g_ref, o_ref):
    o_ref[...] = _rms(x_ref[...].astype(F32), g_ref[...]).astype(o_ref.dtype)


def _rmsnorm(x, g, bm):
    rows, d = x.shape
    blocks = _nbytes((bm, d), F32) + _nbytes((bm, d), BF16)
    return pl.pallas_call(
        _rmsnorm_kernel,
        out_shape=jax.ShapeDtypeStruct((rows, d), BF16),
        grid=(rows // bm,),
        in_specs=[pl.BlockSpec((bm, d), lambda i: (i, 0)),
                  pl.BlockSpec((1, d), lambda i: (0, 0))],
        out_specs=pl.BlockSpec((bm, d), lambda i: (i, 0)),
        compiler_params=_cparams(1, blocks),
    )(x, g)


def _residual_kernel(x_ref, y_ref, gp_ref, gn_ref, xo_ref, ho_ref):
    xn = x_ref[...] + _rms(y_ref[...], gp_ref[...])
    xo_ref[...] = xn
    ho_ref[...] = _rms(xn, gn_ref[...]).astype(ho_ref.dtype)


def _residual_last_kernel(x_ref, y_ref, gp_ref, xo_ref):
    xo_ref[...] = x_ref[...] + _rms(y_ref[...], gp_ref[...])


def _residual(x, y, g_post, g_next):
    rows, d = x.shape
    bm = ROW_BM
    row_spec = pl.BlockSpec((bm, d), lambda i: (i, 0))
    g_spec = pl.BlockSpec((1, d), lambda i: (0, 0))
    if g_next is None:
        blocks = 3 * _nbytes((bm, d), F32)
        return pl.pallas_call(
            _residual_last_kernel,
            out_shape=jax.ShapeDtypeStruct((rows, d), F32),
            grid=(rows // bm,),
            in_specs=[row_spec, row_spec, g_spec],
            out_specs=row_spec,
            compiler_params=_cparams(1, blocks),
        )(x, y, g_post), None
    blocks = 3 * _nbytes((bm, d), F32) + _nbytes((bm, d), BF16)
    return pl.pallas_call(
        _residual_kernel,
        out_shape=(jax.ShapeDtypeStruct((rows, d), F32), jax.ShapeDtypeStruct((rows, d), BF16)),
        grid=(rows // bm,),
        in_specs=[row_spec, row_spec, g_spec, g_spec],
        out_specs=(row_spec, row_spec),
        compiler_params=_cparams(1, blocks),
    )(x, y, g_post, g_next)


def _mm_kernel(a_ref, w_ref, o_ref):
    o_ref[...] = jnp.dot(a_ref[...], w_ref[...], preferred_element_type=F32).astype(o_ref.dtype)


def _matmul(a, w, out_dtype, bm=MM_BM, bn=MM_BN):
    m, k = a.shape
    n = w.shape[1]
    bm, bn = min(bm, m), min(bn, n)
    blocks = _nbytes((bm, k), BF16) + _nbytes((k, bn), BF16) + _nbytes((bm, bn), out_dtype)
    return pl.pallas_call(
        _mm_kernel,
        out_shape=jax.ShapeDtypeStruct((m, n), out_dtype),
        grid=(m // bm, n // bn),
        in_specs=[pl.BlockSpec((bm, k), lambda i, j: (i, 0)),
                  pl.BlockSpec((k, bn), lambda i, j: (0, j))],
        out_specs=pl.BlockSpec((bm, bn), lambda i, j: (i, j)),
        compiler_params=_cparams(2, blocks, _nbytes((bm, bn), F32)),
    )(a, w)


def _mm_gate_up_kernel(a_ref, wg_ref, wu_ref, o_ref):
    a = a_ref[...]
    g = jnp.dot(a, wg_ref[...], preferred_element_type=F32)
    u = jnp.dot(a, wu_ref[...], preferred_element_type=F32)
    o_ref[...] = (_silu(g) * u).astype(o_ref.dtype)


def _gate_up(a, wg, wu):
    m, k = a.shape
    n = wg.shape[1]
    bm, bn = MM_BM, FF_BN
    blocks = _nbytes((bm, k), BF16) + 2 * _nbytes((k, bn), BF16) + _nbytes((bm, bn), BF16)
    w_spec = pl.BlockSpec((k, bn), lambda i, j: (0, j))
    return pl.pallas_call(
        _mm_gate_up_kernel,
        out_shape=jax.ShapeDtypeStruct((m, n), BF16),
        grid=(m // bm, n // bn),
        in_specs=[pl.BlockSpec((bm, k), lambda i, j: (i, 0)), w_spec, w_spec],
        out_specs=pl.BlockSpec((bm, bn), lambda i, j: (i, j)),
        compiler_params=_cparams(2, blocks, 2 * _nbytes((bm, bn), F32)),
    )(a, wg, wu)


def _mm_acc_kernel(a_ref, w_ref, o_ref, acc_ref):
    kk = pl.program_id(2)

    @pl.when(kk == 0)
    def _():
        acc_ref[...] = jnp.zeros_like(acc_ref)

    acc_ref[...] += jnp.dot(a_ref[...], w_ref[...], preferred_element_type=F32)

    @pl.when(kk == pl.num_programs(2) - 1)
    def _():
        o_ref[...] = acc_ref[...].astype(o_ref.dtype)


def _matmul_kgrid(a, w, out_dtype, bk):
    m, k = a.shape
    n = w.shape[1]
    bm, bn = MM_BM, MM_BN
    blocks = _nbytes((bm, bk), BF16) + _nbytes((bk, bn), BF16) + _nbytes((bm, bn), out_dtype)
    acc = _nbytes((bm, bn), F32)
    return pl.pallas_call(
        _mm_acc_kernel,
        out_shape=jax.ShapeDtypeStruct((m, n), out_dtype),
        grid=(m // bm, n // bn, k // bk),
        in_specs=[pl.BlockSpec((bm, bk), lambda i, j, kk: (i, kk)),
                  pl.BlockSpec((bk, bn), lambda i, j, kk: (kk, j))],
        out_specs=pl.BlockSpec((bm, bn), lambda i, j, kk: (i, j)),
        scratch_shapes=[pltpu.VMEM((bm, bn), F32)],
        compiler_params=_cparams(3, blocks, 2 * acc, reduction_last=True),
    )(a, w)


def _mm_group_kernel(a_ref, w_ref, s_ref, o_ref):
    y = jnp.dot(a_ref[...], w_ref[...], preferred_element_type=F32)
    o_ref[...] = (y * s_ref[...]).astype(o_ref.dtype)


def _group_matmul(p, gw, scale):
    m = p.shape[0]
    g, dg, _ = gw.shape
    bm = MM_BM
    blocks = _nbytes((bm, dg), BF16) * 2 + _nbytes((dg, dg), BF16)
    return pl.pallas_call(
        _mm_group_kernel,
        out_shape=jax.ShapeDtypeStruct((m, g * dg), BF16),
        grid=(g, m // bm),
        in_specs=[pl.BlockSpec((bm, dg), lambda gi, i: (i, gi)),
                  pl.BlockSpec((None, dg, dg), lambda gi, i: (gi, 0, 0)),
                  pl.BlockSpec((1, dg), lambda gi, i: (0, gi))],
        out_specs=pl.BlockSpec((bm, dg), lambda gi, i: (i, gi)),
        compiler_params=_cparams(2, blocks, _nbytes((bm, dg), F32)),
    )(p, gw, scale)


def _a_mixer_kernel(u_ref, v_ref, ws_ref, bs_ref, gv_ref, o_ref):
    t_idx = lax.broadcasted_iota(jnp.int32, (CHUNK, CHUNK), 0)
    s_idx = lax.broadcasted_iota(jnp.int32, (CHUNK, CHUNK), 1)
    w = jnp.where(s_idx <= t_idx, ws_ref[...], 0.0).astype(BF16)
    bias = bs_ref[...]
    gv = gv_ref[...]
    for c in range(MIX_BM // CHUNK):
        rows = pl.ds(c * CHUNK, CHUNK)
        v = _layer_norm(jax.nn.gelu(v_ref[rows, :]), gv).astype(BF16)
        vm = jnp.dot(w, v, preferred_element_type=F32) + bias
        o_ref[rows, :] = (jax.nn.gelu(u_ref[rows, :]) * vm).astype(o_ref.dtype)


def _a_mixer(z, w_s, b_s, g_v):
    t = z.shape[0]
    hd = A_HEAD_DIM
    blocks = 2 * _nbytes((MIX_BM, hd), F32) + _nbytes((MIX_BM, hd), BF16)
    return pl.pallas_call(
        _a_mixer_kernel,
        out_shape=jax.ShapeDtypeStruct((t, D_A), BF16),
        grid=(t // MIX_BM, A_HEADS),
        in_specs=[pl.BlockSpec((MIX_BM, hd), lambda i, h: (i, h)),
                  pl.BlockSpec((MIX_BM, hd), lambda i, h: (i, A_HEADS + h)),
                  pl.BlockSpec((None, CHUNK, CHUNK), lambda i, h: (h, 0, 0)),
                  pl.BlockSpec((None, CHUNK, 1), lambda i, h: (h, 0, 0)),
                  pl.BlockSpec((None, 1, hd), lambda i, h: (h, 0, 0))],
        out_specs=pl.BlockSpec((MIX_BM, hd), lambda i, h: (i, h)),
        compiler_params=_cparams(2, blocks),
    )(z, z, w_s, b_s.reshape(A_HEADS, CHUNK, 1), g_v.reshape(A_HEADS, 1, hd))


def _b_mixer_kernel(a_ref, g_ref, ha_ref, hg_ref, cw_ref, cb_ref, gn_ref, o_ref, zbuf):
    i = pl.program_id(0)
    at_seq_start = lax.rem(i * MIX_BM, SEQ) == 0
    halo = ha_ref[...] * _sigmoid(hg_ref[...])
    zbuf[pl.ds(0, CONV_HALO), :] = jnp.where(at_seq_start, 0.0, halo)
    zbuf[pl.ds(CONV_HALO, MIX_BM), :] = a_ref[...] * _sigmoid(g_ref[...])
    cb = cb_ref[...]
    gn = gn_ref[...]
    first_tap = CONV_HALO - (CONV_WIDTH - 1)
    for r in range(MIX_BM // ROW_CHUNK):
        r0 = r * ROW_CHUNK
        acc = jnp.zeros((ROW_CHUNK, B_GROUP_DIM), F32)
        for k in range(CONV_WIDTH):
            acc = acc + cw_ref[pl.ds(k, 1), :] * zbuf[pl.ds(r0 + first_tap + k, ROW_CHUNK), :]
        y = _layer_norm(acc + cb, gn)
        o_ref[pl.ds(r0, ROW_CHUNK), :] = _silu(y).astype(o_ref.dtype)


def _b_mixer(z, conv_w, conv_b, gn_g):
    t = z.shape[0]
    gd = B_GROUP_DIM
    a_blk = 2 * D_A // gd
    g_blk = a_blk + D_B // gd
    halo_per_blk = MIX_BM // CONV_HALO

    def halo_map(col0):
        return lambda i, g: (jnp.maximum(i * halo_per_blk - 1, 0), col0 + g)

    blocks = (2 * _nbytes((MIX_BM, gd), F32) + 2 * _nbytes((CONV_HALO, gd), F32)
              + _nbytes((MIX_BM, gd), BF16))
    scratch = _nbytes((MIX_BM + CONV_HALO, gd), F32)
    return pl.pallas_call(
        _b_mixer_kernel,
        out_shape=jax.ShapeDtypeStruct((t, D_B), BF16),
        grid=(t // MIX_BM, B_GROUPS),
        in_specs=[pl.BlockSpec((MIX_BM, gd), lambda i, g: (i, a_blk + g)),
                  pl.BlockSpec((MIX_BM, gd), lambda i, g: (i, g_blk + g)),
                  pl.BlockSpec((CONV_HALO, gd), halo_map(a_blk)),
                  pl.BlockSpec((CONV_HALO, gd), halo_map(g_blk)),
                  pl.BlockSpec((CONV_WIDTH, gd), lambda i, g: (0, g)),
                  pl.BlockSpec((1, gd), lambda i, g: (0, g)),
                  pl.BlockSpec((1, gd), lambda i, g: (0, g))],
        out_specs=pl.BlockSpec((MIX_BM, gd), lambda i, g: (i, g)),
        scratch_shapes=[pltpu.VMEM((MIX_BM + CONV_HALO, gd), F32)],
        compiler_params=_cparams(2, blocks, scratch),
    )(z, z, z, z, conv_w, conv_b.reshape(1, D_B), gn_g.reshape(1, D_B))


POOL_BC = 256


def _pool_kernel(p_ref, h_ref, o_ref, zbuf):
    i = pl.program_id(0)
    g = pl.program_id(1)
    at_seq_start = lax.rem(i * MIX_BM, SEQ) == 0
    zbuf[pl.ds(0, POOL_HALO), :] = jnp.where(at_seq_start, 0.0, h_ref[...])
    zbuf[pl.ds(POOL_HALO, MIX_BM), :] = p_ref[...]
    t0 = lax.rem(i * MIX_BM, SEQ)

    for gi, win in enumerate(POOL_WINDOWS):

        @pl.when(g == gi)
        def _(win=win):
            for r in range(MIX_BM // ROW_CHUNK):
                r0 = r * ROW_CHUNK
                cur = zbuf[pl.ds(POOL_HALO + r0, ROW_CHUNK), :]
                s = cur
                for j in range(1, win):
                    s = s + zbuf[pl.ds(POOL_HALO + r0 - j, ROW_CHUNK), :]
                t = t0 + r0 + lax.broadcasted_iota(jnp.int32, (ROW_CHUNK, 1), 0)
                count = jnp.minimum(t + 1, win).astype(F32)
                o_ref[pl.ds(r0, ROW_CHUNK), :] = (s / count - cur).astype(o_ref.dtype)


def _pool(p):
    t, d = p.shape
    cols_per_group = C_GROUP_DIM // POOL_BC
    halo_per_blk = MIX_BM // POOL_HALO
    blocks = (_nbytes((MIX_BM, POOL_BC), F32) + _nbytes((POOL_HALO, POOL_BC), F32)
              + _nbytes((MIX_BM, POOL_BC), BF16))
    scratch = _nbytes((MIX_BM + POOL_HALO, POOL_BC), F32)
    return pl.pallas_call(
        _pool_kernel,
        out_shape=jax.ShapeDtypeStruct((t, d), BF16),
        grid=(t // MIX_BM, C_GROUPS, cols_per_group),
        in_specs=[pl.BlockSpec((MIX_BM, POOL_BC), lambda i, g, c: (i, g * cols_per_group + c)),
                  pl.BlockSpec((POOL_HALO, POOL_BC),
                               lambda i, g, c: (jnp.maximum(i * halo_per_blk - 1, 0),
                                                g * cols_per_group + c))],
        out_specs=pl.BlockSpec((MIX_BM, POOL_BC), lambda i, g, c: (i, g * cols_per_group + c)),
        scratch_shapes=[pltpu.VMEM((MIX_BM + POOL_HALO, POOL_BC), F32)],
        compiler_params=_cparams(3, blocks, scratch),
    )(p, p)


def _attn_kernel(q_ref, k_ref, v_ref, o_ref):
    s = lax.dot_general(q_ref[...], k_ref[...], (((1,), (1,)), ((), ())),
                        preferred_element_type=F32) * (X_HEAD_DIM ** -0.5)
    e = jnp.exp(s - jnp.max(s, axis=-1, keepdims=True))
    p = e / jnp.sum(e, axis=-1, keepdims=True)
    o_ref[...] = jnp.dot(p.astype(BF16), v_ref[...], preferred_element_type=F32).astype(o_ref.dtype)


def _attention(q, k, v):
    t = q.shape[0]
    q_blocks_per_batch = SEQ // ATT_BQ
    hd = X_HEAD_DIM
    blocks = 2 * _nbytes((ATT_BQ, hd), BF16) + 2 * _nbytes((MEM_LEN, hd), BF16)
    kv_spec = pl.BlockSpec((MEM_LEN, hd), lambda i, h: (i // q_blocks_per_batch, h))
    return pl.pallas_call(
        _attn_kernel,
        out_shape=jax.ShapeDtypeStruct((t, X_DIM), BF16),
        grid=(t // ATT_BQ, X_HEADS),
        in_specs=[pl.BlockSpec((ATT_BQ, hd), lambda i, h: (i, h)), kv_spec, kv_spec],
        out_specs=pl.BlockSpec((ATT_BQ, hd), lambda i, h: (i, h)),
        compiler_params=_cparams(2, blocks, 2 * _nbytes((ATT_BQ, MEM_LEN), F32)),
    )(q, k, v)


def _pad_cols(w, n):
    return jnp.pad(w, ((0, 0), (0, n - w.shape[1])))


def _pad_rows(w, n):
    return jnp.pad(w, ((0, n - w.shape[0]), (0, 0)))


def kernel(x, mem, mix_pre_g, mix_post_g, xattn_pre_g, mem_norm_g, xattn_post_g, ffn_pre_g, ffn_post_g, ab_w_in, a_w_s, a_b_s, a_v_norm_g, b_conv_w, b_conv_b, b_gn_g, ab_w_out, c_w_in, c_group_w, c_scale, c_w_out, xq_w, xk_w, xv_w, xo_w, ffn_gate_w, ffn_up_w, ffn_down_w):
    def gain(g, i):
        return g[i].reshape(1, D_MODEL)

    xf = x.reshape(TOKENS, D_MODEL)
    memf = mem.reshape(BATCH * MEM_LEN, D_MODEL)
    h = _rmsnorm(xf, gain(mix_pre_g, 0), ROW_BM)

    for i in range(DEPTH):
        j = i // 2
        if i % 2 == 0:
            z = _matmul(h, ab_w_in[j].astype(BF16), F32)
            ya = _a_mixer(z, a_w_s[j], a_b_s[j], a_v_norm_g[j])
            yb = _b_mixer(z, b_conv_w[j], b_conv_b[j], b_gn_g[j])
            y = _matmul(jnp.concatenate([ya, yb], axis=-1), ab_w_out[j].astype(BF16), F32)
        else:
            p = _pool(_matmul(h, c_w_in[j].astype(BF16), F32))
            yg = _group_matmul(p, c_group_w[j].astype(BF16), c_scale[j].reshape(1, D_MODEL))
            y = _matmul(yg, c_w_out[j].astype(BF16), F32)
        xf, h = _residual(xf, y, gain(mix_post_g, i), gain(xattn_pre_g, i))

        m = _rmsnorm(memf, gain(mem_norm_g, i), ROW_BM)
        q = _matmul(h, xq_w[i].astype(BF16), BF16)
        k = _matmul(m, xk_w[i].astype(BF16), BF16)
        v = _matmul(m, xv_w[i].astype(BF16), BF16)
        o = _attention(q, k, v)
        y = _matmul(o, xo_w[i].astype(BF16), F32)
        xf, h = _residual(xf, y, gain(xattn_post_g, i), gain(ffn_pre_g, i))

        wg = _pad_cols(ffn_gate_w[i].astype(BF16), D_FF_PAD)
        wu = _pad_cols(ffn_up_w[i].astype(BF16), D_FF_PAD)
        wd = _pad_rows(ffn_down_w[i].astype(BF16), D_FF_PAD)
        y = _matmul_kgrid(_gate_up(h, wg, wu), wd, F32, FF_BK)
        g_next = gain(mix_pre_g, i + 1) if i + 1 < DEPTH else None
        xf, h = _residual(xf, y, gain(ffn_post_g, i), g_next)

    return xf.reshape(BATCH, SEQ, D_MODEL)
```

```python
import functools

import jax
import jax.numpy as jnp
from jax import lax
from jax.experimental import pallas as pl
from jax.experimental.pallas import tpu as pltpu

D_MODEL = 4096
BATCH = 2
SEQ = 4096
DEPTH = 4
TOKENS = BATCH * SEQ
MEM_LEN = 256
EPS = 1e-6
D_A = D_MODEL // 2
D_B = D_MODEL // 2
A_HEADS = 8
A_HEAD_DIM = D_A // A_HEADS
CHUNK = 128
B_GROUPS = 8
B_GROUP_DIM = D_B // B_GROUPS
CONV_WIDTH = 31
C_GROUPS = 4
C_GROUP_DIM = D_MODEL // C_GROUPS
POOL_WINDOWS = (2, 4, 8, 16)
X_HEADS = 4
X_HEAD_DIM = 512
X_DIM = X_HEADS * X_HEAD_DIM
D_FF = -(-8 * D_MODEL // (3 * 256)) * 256

F32 = jnp.float32
BF16 = jnp.bfloat16

V7X_VMEM_BYTES = 64 * 1024 * 1024
V7X_SUBLANES = 8
MIB = 1024 * 1024

MM_BM = 1024
MM_BN = 1024
PAIR_BN = 512
FF_BM = 2048
FF_BN = 256
DOWN_BM = 512
DOWN_BN = 512
ROW_BM = 256
MIX_BM = 512
CONV_HALO = 32
POOL_HALO = 16
ROW_CHUNK = 64
POOL_BC = 256
ATT_BQ = 512


def _nbytes(shape, dtype):
    n = 1
    for s in shape:
        n *= s
    return n * jnp.dtype(dtype).itemsize


def _vmem_limit(pipelined_bytes, scratch_bytes=0, temp_bytes=0):
    need = (2 * pipelined_bytes + scratch_bytes + temp_bytes) * 5 // 4
    return int(min(max(need, 16 * MIB), V7X_VMEM_BYTES - 2 * MIB))


def _sigmoid(x):
    return 1.0 / (1.0 + jnp.exp(-x))


def _silu(x):
    return x * _sigmoid(x)


def _rms(x, g):
    return x * lax.rsqrt(jnp.mean(x * x, axis=-1, keepdims=True) + EPS) * g


def _layer_norm(x, g):
    mu = jnp.mean(x, axis=-1, keepdims=True)
    xc = x - mu
    var = jnp.mean(xc * xc, axis=-1, keepdims=True)
    return xc * lax.rsqrt(var + EPS) * g


def _rmsnorm_kernel(x_ref, g_ref, o_ref):
    o_ref[...] = _rms(x_ref[...].astype(F32), g_ref[...]).astype(o_ref.dtype)


def _rmsnorm(x, g, name):
    rows, d = x.shape
    bm = ROW_BM
    blocks = _nbytes((bm, d), F32) + _nbytes((bm, d), BF16)
    return pl.pallas_call(
        _rmsnorm_kernel,
        out_shape=jax.ShapeDtypeStruct((rows, d), BF16),
        grid=(rows // bm,),
        in_specs=[pl.BlockSpec((bm, d), lambda i: (i, 0)),
                  pl.BlockSpec((1, d), lambda i: (0, 0))],
        out_specs=pl.BlockSpec((bm, d), lambda i: (i, 0)),
        compiler_params=pltpu.CompilerParams(
            dimension_semantics=("parallel",),
            vmem_limit_bytes=_vmem_limit(blocks, 0, _nbytes((bm, d), F32))),
        name=name,
    )(x, g)


def _residual_kernel(x_ref, y_ref, gp_ref, gn_ref, xo_ref, ho_ref):
    xn = x_ref[...] + _rms(y_ref[...], gp_ref[...])
    xo_ref[...] = xn
    ho_ref[...] = _rms(xn, gn_ref[...]).astype(ho_ref.dtype)


def _residual_last_kernel(x_ref, y_ref, gp_ref, xo_ref):
    xo_ref[...] = x_ref[...] + _rms(y_ref[...], gp_ref[...])


def _residual(x, y, g_post, g_next):
    rows, d = x.shape
    bm = ROW_BM
    row_spec = pl.BlockSpec((bm, d), lambda i: (i, 0))
    g_spec = pl.BlockSpec((1, d), lambda i: (0, 0))
    f32_blk = _nbytes((bm, d), F32)
    if g_next is None:
        return pl.pallas_call(
            _residual_last_kernel,
            out_shape=jax.ShapeDtypeStruct((rows, d), F32),
            grid=(rows // bm,),
            in_specs=[row_spec, row_spec, g_spec],
            out_specs=row_spec,
            compiler_params=pltpu.CompilerParams(
                dimension_semantics=("parallel",),
                vmem_limit_bytes=_vmem_limit(3 * f32_blk, 0, f32_blk)),
            name="residual_last",
        )(x, y, g_post), None
    return pl.pallas_call(
        _residual_kernel,
        out_shape=(jax.ShapeDtypeStruct((rows, d), F32), jax.ShapeDtypeStruct((rows, d), BF16)),
        grid=(rows // bm,),
        in_specs=[row_spec, row_spec, g_spec, g_spec],
        out_specs=(row_spec, row_spec),
        compiler_params=pltpu.CompilerParams(
            dimension_semantics=("parallel",),
            vmem_limit_bytes=_vmem_limit(3 * f32_blk + f32_blk // 2, 0, f32_blk)),
        name="residual_norm",
    )(x, y, g_post, g_next)


def _ws_kernel(*refs, n_arr, n_extra, n_out, wsrc, kc, nm, bn, epilogue):
    a_ref = refs[0]
    w_hbm = refs[1:1 + n_arr]
    extra = refs[1 + n_arr:1 + n_arr + n_extra]
    outs = refs[1 + n_arr + n_extra:1 + n_arr + n_extra + n_out]
    wbf, stage, sem = refs[1 + n_arr + n_extra + n_out:]
    n_w = len(wsrc)
    jb = pl.program_id(0)
    i = pl.program_id(1)
    slot = lax.rem(jb, 2)

    def chunk_copy(w, block, c):
        arr, lead, row0, col0, row_stride, col_stride = wsrc[w]
        rows = pl.ds(row0 + block * row_stride + c * kc, kc)
        cols = pl.ds(col0 + block * col_stride, bn)
        return pltpu.make_async_copy(w_hbm[arr].at[(*lead, rows, cols)], stage.at[w], sem.at[w])

    def land(w, dst_slot, c):
        r0 = c * kc if isinstance(c, int) else pl.multiple_of(c * kc, kc)
        wbf[dst_slot, w, pl.ds(r0, kc), :] = stage[w].astype(BF16)

    @pl.when((jb == 0) & (i == 0))
    def _():
        for c in range(nm):
            for w in range(n_w):
                cp = chunk_copy(w, 0, c)
                cp.start()
                cp.wait()
                land(w, 0, c)

    has_next = jb + 1 < pl.num_programs(0)

    @pl.when(has_next)
    def _():
        for w in range(n_w):
            chunk_copy(w, jb + 1, i).start()

    a = a_ref[...]
    accs = [jnp.dot(a, wbf[slot, w], preferred_element_type=F32) for w in range(n_w)]
    results = epilogue(accs, extra)
    for o_ref, r in zip(outs, results):
        o_ref[...] = r.astype(o_ref.dtype)

    @pl.when(has_next)
    def _():
        for w in range(n_w):
            chunk_copy(w, jb + 1, i).wait()
            land(w, 1 - slot, i)


def _ws_matmul(a, weights, wsrc, *, k, n, bm, bn, out_dtypes, epilogue, name,
               extras=(), a_follows_block=False):
    m, ka = a.shape
    nm = m // bm
    kc = k // nm
    n_w = len(wsrc)
    assert m % bm == 0 and n % bn == 0 and k % nm == 0 and kc % (2 * V7X_SUBLANES) == 0
    assert ka == (k * (n // bn) if a_follows_block else k)
    a_spec = pl.BlockSpec((bm, k), (lambda jb, i: (i, jb)) if a_follows_block else (lambda jb, i: (i, 0)))
    out_spec = pl.BlockSpec((bm, bn), lambda jb, i: (i, jb))
    pipelined = (_nbytes((bm, k), BF16) + sum(_nbytes((bm, bn), d) for d in out_dtypes)
                 + len(extras) * _nbytes((V7X_SUBLANES, bn), F32))
    scratch = 2 * n_w * _nbytes((k, bn), BF16) + n_w * _nbytes((kc, bn), F32)
    temps = (n_w + 1) * _nbytes((bm, bn), F32)
    kern = functools.partial(_ws_kernel, n_arr=len(weights), n_extra=len(extras), n_out=len(out_dtypes),
                             wsrc=wsrc, kc=kc, nm=nm, bn=bn, epilogue=epilogue)
    return pl.pallas_call(
        kern,
        out_shape=tuple(jax.ShapeDtypeStruct((m, n), d) for d in out_dtypes),
        grid=(n // bn, nm),
        in_specs=([a_spec] + [pl.BlockSpec(memory_space=pl.ANY)] * len(weights)
                  + [pl.BlockSpec((1, bn), lambda jb, i: (0, jb))] * len(extras)),
        out_specs=tuple(out_spec for _ in out_dtypes),
        scratch_shapes=[pltpu.VMEM((2, n_w, k, bn), BF16),
                        pltpu.VMEM((n_w, kc, bn), F32),
                        pltpu.SemaphoreType.DMA((n_w,))],
        compiler_params=pltpu.CompilerParams(
            dimension_semantics=("arbitrary", "arbitrary"),
            vmem_limit_bytes=_vmem_limit(pipelined, scratch, temps)),
        name=name,
    )(a, *weights, *extras)


def _ep_identity(accs, extra):
    return (accs[0],)


def _ep_silu_mul(accs, extra):
    return (_silu(accs[0]) * accs[1],)


def _ep_glu(accs, extra):
    return (accs[0] * _sigmoid(accs[1]),)


def _ep_pair(accs, extra):
    return (accs[0], accs[1])


def _ep_scale(accs, extra):
    return (accs[0] * extra[0][...],)


def _ep_gating_inputs(accs, extra):
    gv = extra[0][...]
    v = jax.nn.gelu(accs[1])
    heads = []
    for h in range(v.shape[1] // A_HEAD_DIM):
        cols = slice(h * A_HEAD_DIM, (h + 1) * A_HEAD_DIM)
        heads.append(_layer_norm(v[:, cols], gv[:, cols]))
    return (jax.nn.gelu(accs[0]), jnp.concatenate(heads, axis=1))


def _dense(a, w, lead, *, k, n, out_dtype, name, bm=MM_BM, bn=MM_BN):
    return _ws_matmul(a, [w], ((0, lead, 0, 0, 0, bn),), k=k, n=n, bm=bm, bn=bn,
                      out_dtypes=(out_dtype,), epilogue=_ep_identity, name=name)[0]


def _a_mixer_kernel(u_ref, v_ref, ws_ref, bs_ref, o_ref):
    t_idx = lax.broadcasted_iota(jnp.int32, (CHUNK, CHUNK), 0)
    s_idx = lax.broadcasted_iota(jnp.int32, (CHUNK, CHUNK), 1)
    w = jnp.where(s_idx <= t_idx, ws_ref[...], 0.0).astype(BF16)
    bias = bs_ref[...]
    for c in range(MIX_BM // CHUNK):
        rows = pl.ds(c * CHUNK, CHUNK)
        vm = jnp.dot(w, v_ref[rows, :], preferred_element_type=F32) + bias
        o_ref[rows, :] = (u_ref[rows, :] * vm).astype(o_ref.dtype)


def _a_mixer(ug, vln, w_s, b_s):
    t = ug.shape[0]
    hd = A_HEAD_DIM
    blocks = _nbytes((MIX_BM, hd), F32) + 2 * _nbytes((MIX_BM, hd), BF16)
    return pl.pallas_call(
        _a_mixer_kernel,
        out_shape=jax.ShapeDtypeStruct((t, D_A), BF16),
        grid=(t // MIX_BM, A_HEADS),
        in_specs=[pl.BlockSpec((MIX_BM, hd), lambda i, h: (i, h)),
                  pl.BlockSpec((MIX_BM, hd), lambda i, h: (i, h)),
                  pl.BlockSpec((None, CHUNK, CHUNK), lambda i, h: (h, 0, 0)),
                  pl.BlockSpec((None, CHUNK, 1), lambda i, h: (h, 0, 0))],
        out_specs=pl.BlockSpec((MIX_BM, hd), lambda i, h: (i, h)),
        compiler_params=pltpu.CompilerParams(
            dimension_semantics=("parallel", "parallel"),
            vmem_limit_bytes=_vmem_limit(blocks)),
        name="gating_mixer",
    )(ug, vln, w_s, b_s.reshape(A_HEADS, CHUNK, 1))


_CONV_TAP0 = CONV_HALO - (CONV_WIDTH - 1)


def _b_mixer_kernel(z_ref, hz_ref, cw_ref, cb_ref, gn_ref, o_ref, zbuf):
    i = pl.program_id(0)
    at_seq_start = lax.rem(i * MIX_BM, SEQ) == 0
    zbuf[pl.ds(0, CONV_HALO), :] = jnp.where(at_seq_start, 0.0, hz_ref[...])
    zbuf[pl.ds(CONV_HALO, MIX_BM), :] = z_ref[...]
    cb = cb_ref[...]
    gn = gn_ref[...]
    for r in range(MIX_BM // ROW_CHUNK):
        r0 = r * ROW_CHUNK
        acc = jnp.zeros((ROW_CHUNK, B_GROUP_DIM), F32)
        for phase in range(V7X_SUBLANES):
            taps = [k for k in range(CONV_WIDTH) if (_CONV_TAP0 + k) % V7X_SUBLANES == phase]
            lo = _CONV_TAP0 + taps[0]
            span = ROW_CHUNK + (taps[-1] - taps[0])
            slab = zbuf[pl.ds(r0 + lo, span), :]
            for k in taps:
                off = k - taps[0]
                acc = acc + cw_ref[pl.ds(k, 1), :] * slab[off:off + ROW_CHUNK, :]
        y = _layer_norm(acc + cb, gn)
        o_ref[pl.ds(r0, ROW_CHUNK), :] = _silu(y).astype(o_ref.dtype)


def _b_mixer(z, conv_w, conv_b, gn_g):
    t = z.shape[0]
    gd = B_GROUP_DIM
    halo_per_blk = MIX_BM // CONV_HALO
    blocks = (_nbytes((MIX_BM, gd), F32) + _nbytes((CONV_HALO, gd), F32) + _nbytes((MIX_BM, gd), BF16)
              + _nbytes((CONV_HALO, gd), F32))
    scratch = _nbytes((MIX_BM + CONV_HALO, gd), F32)
    return pl.pallas_call(
        _b_mixer_kernel,
        out_shape=jax.ShapeDtypeStruct((t, D_B), BF16),
        grid=(t // MIX_BM, B_GROUPS),
        in_specs=[pl.BlockSpec((MIX_BM, gd), lambda i, g: (i, g)),
                  pl.BlockSpec((CONV_HALO, gd), lambda i, g: (jnp.maximum(i * halo_per_blk - 1, 0), g)),
                  pl.BlockSpec((CONV_WIDTH, gd), lambda i, g: (0, g)),
                  pl.BlockSpec((1, gd), lambda i, g: (0, g)),
                  pl.BlockSpec((1, gd), lambda i, g: (0, g))],
        out_specs=pl.BlockSpec((MIX_BM, gd), lambda i, g: (i, g)),
        scratch_shapes=[pltpu.VMEM((MIX_BM + CONV_HALO, gd), F32)],
        compiler_params=pltpu.CompilerParams(
            dimension_semantics=("parallel", "parallel"),
            vmem_limit_bytes=_vmem_limit(blocks, scratch)),
        name="conv_mixer",
    )(z, z, conv_w, conv_b.reshape(1, D_B), gn_g.reshape(1, D_B))


def _pool_kernel(p_ref, h_ref, o_ref, zbuf):
    i = pl.program_id(0)
    g = pl.program_id(1)
    t0 = lax.rem(i * MIX_BM, SEQ)
    zbuf[pl.ds(0, POOL_HALO), :] = jnp.where(t0 == 0, 0.0, h_ref[...])
    zbuf[pl.ds(POOL_HALO, MIX_BM), :] = p_ref[...]

    for gi, win in enumerate(POOL_WINDOWS):

        @pl.when(g == gi)
        def _(win=win):
            for r in range(MIX_BM // ROW_CHUNK):
                r0 = r * ROW_CHUNK
                cur = zbuf[pl.ds(POOL_HALO + r0, ROW_CHUNK), :]
                s = cur
                for j in range(1, win):
                    s = s + zbuf[pl.ds(POOL_HALO + r0 - j, ROW_CHUNK), :]
                t = t0 + r0 + lax.broadcasted_iota(jnp.int32, (ROW_CHUNK, 1), 0)
                count = jnp.minimum(t + 1, win).astype(F32)
                o_ref[pl.ds(r0, ROW_CHUNK), :] = (s / count - cur).astype(o_ref.dtype)


def _pool(p):
    t, d = p.shape
    cols_per_group = C_GROUP_DIM // POOL_BC
    halo_per_blk = MIX_BM // POOL_HALO
    blocks = (_nbytes((MIX_BM, POOL_BC), F32) + _nbytes((POOL_HALO, POOL_BC), F32)
              + _nbytes((MIX_BM, POOL_BC), BF16))
    scratch = _nbytes((MIX_BM + POOL_HALO, POOL_BC), F32)
    return pl.pallas_call(
        _pool_kernel,
        out_shape=jax.ShapeDtypeStruct((t, d), BF16),
        grid=(t // MIX_BM, C_GROUPS, cols_per_group),
        in_specs=[pl.BlockSpec((MIX_BM, POOL_BC), lambda i, g, c: (i, g * cols_per_group + c)),
                  pl.BlockSpec((POOL_HALO, POOL_BC),
                               lambda i, g, c: (jnp.maximum(i * halo_per_blk - 1, 0),
                                                g * cols_per_group + c))],
        out_specs=pl.BlockSpec((MIX_BM, POOL_BC), lambda i, g, c: (i, g * cols_per_group + c)),
        scratch_shapes=[pltpu.VMEM((MIX_BM + POOL_HALO, POOL_BC), F32)],
        compiler_params=pltpu.CompilerParams(
            dimension_semantics=("parallel", "parallel", "parallel"),
            vmem_limit_bytes=_vmem_limit(blocks, scratch)),
        name="pool_mixer",
    )(p, p)


def _attn_kernel(q_ref, k_ref, v_ref, o_ref):
    s = lax.dot_general(q_ref[...], k_ref[...], (((1,), (1,)), ((), ())),
                        preferred_element_type=F32) * (X_HEAD_DIM ** -0.5)
    e = jnp.exp(s - jnp.max(s, axis=-1, keepdims=True))
    p = e / jnp.sum(e, axis=-1, keepdims=True)
    o_ref[...] = jnp.dot(p.astype(BF16), v_ref[...], preferred_element_type=F32).astype(o_ref.dtype)


def _attention(q, k, v):
    t = q.shape[0]
    q_blocks_per_batch = SEQ // ATT_BQ
    hd = X_HEAD_DIM
    blocks = 2 * _nbytes((ATT_BQ, hd), BF16) + 2 * _nbytes((MEM_LEN, hd), BF16)
    kv_spec = pl.BlockSpec((MEM_LEN, hd), lambda i, h: (i // q_blocks_per_batch, h))
    return pl.pallas_call(
        _attn_kernel,
        out_shape=jax.ShapeDtypeStruct((t, X_DIM), BF16),
        grid=(t // ATT_BQ, X_HEADS),
        in_specs=[pl.BlockSpec((ATT_BQ, hd), lambda i, h: (i, h)), kv_spec, kv_spec],
        out_specs=pl.BlockSpec((ATT_BQ, hd), lambda i, h: (i, h)),
        compiler_params=pltpu.CompilerParams(
            dimension_semantics=("parallel", "parallel"),
            vmem_limit_bytes=_vmem_limit(blocks, 0, 4 * _nbytes((ATT_BQ, MEM_LEN), F32))),
        name="mem_attention",
    )(q, k, v)


def kernel(x, mem, mix_pre_g, mix_post_g, xattn_pre_g, mem_norm_g, xattn_post_g, ffn_pre_g, ffn_post_g, ab_w_in, a_w_s, a_b_s, a_v_norm_g, b_conv_w, b_conv_b, b_gn_g, ab_w_out, c_w_in, c_group_w, c_scale, c_w_out, xq_w, xk_w, xv_w, xo_w, ffn_gate_w, ffn_up_w, ffn_down_w):
    def gain(g, i):
        return g[i].reshape(1, D_MODEL)

    xf = x.reshape(TOKENS, D_MODEL)
    memf = mem.reshape(BATCH * MEM_LEN, D_MODEL)
    group_w = c_group_w.reshape(DEPTH // 2, D_MODEL, C_GROUP_DIM)
    h = _rmsnorm(xf, gain(mix_pre_g, 0), "pre_norm")

    for i in range(DEPTH):
        j = i // 2
        if i % 2 == 0:
            bn = PAIR_BN
            ug, vln = _ws_matmul(
                h, [ab_w_in], ((0, (j,), 0, 0, 0, bn), (0, (j,), 0, D_A, 0, bn)),
                k=D_MODEL, n=D_A, bm=MM_BM, bn=bn, out_dtypes=(F32, BF16), epilogue=_ep_gating_inputs,
                extras=(a_v_norm_g[j].reshape(1, D_A),), name="mix_in_gating")
            (zb,) = _ws_matmul(
                h, [ab_w_in], ((0, (j,), 0, 2 * D_A, 0, bn), (0, (j,), 0, 2 * D_A + D_B, 0, bn)),
                k=D_MODEL, n=D_B, bm=MM_BM, bn=bn, out_dtypes=(F32,), epilogue=_ep_glu, name="mix_in_glu")
            ya = _a_mixer(ug, vln, a_w_s[j], a_b_s[j])
            yb = _b_mixer(zb, b_conv_w[j], b_conv_b[j], b_gn_g[j])
            y = _dense(jnp.concatenate([ya, yb], axis=-1), ab_w_out, (j,),
                       k=D_A + D_B, n=D_MODEL, out_dtype=F32, name="mix_out")
        else:
            p = _pool(_dense(h, c_w_in, (j,), k=D_MODEL, n=D_MODEL, out_dtype=F32, name="pool_in"))
            (yg,) = _ws_matmul(
                p, [group_w], ((0, (j,), 0, 0, C_GROUP_DIM, 0),),
                k=C_GROUP_DIM, n=D_MODEL, bm=MM_BM, bn=C_GROUP_DIM, out_dtypes=(BF16,), epilogue=_ep_scale,
                extras=(c_scale[j].reshape(1, D_MODEL),), a_follows_block=True, name="pool_group")
            y = _dense(yg, c_w_out, (j,), k=D_MODEL, n=D_MODEL, out_dtype=F32, name="pool_out")
        xf, h = _residual(xf, y, gain(mix_post_g, i), gain(xattn_pre_g, i))

        m = _rmsnorm(memf, gain(mem_norm_g, i), "mem_norm")
        q = _dense(h, xq_w, (i,), k=D_MODEL, n=X_DIM, out_dtype=BF16, name="attn_q")
        kk, vv = _ws_matmul(
            m, [xk_w, xv_w], ((0, (i,), 0, 0, 0, PAIR_BN), (1, (i,), 0, 0, 0, PAIR_BN)),
            k=D_MODEL, n=X_DIM, bm=BATCH * MEM_LEN, bn=PAIR_BN, out_dtypes=(BF16, BF16), epilogue=_ep_pair,
            name="attn_kv")
        o = _attention(q, kk, vv)
        y = _dense(o, xo_w, (i,), k=X_DIM, n=D_MODEL, out_dtype=F32, name="attn_out")
        xf, h = _residual(xf, y, gain(xattn_post_g, i), gain(ffn_pre_g, i))

        (hid,) = _ws_matmul(
            h, [ffn_gate_w, ffn_up_w], ((0, (i,), 0, 0, 0, FF_BN), (1, (i,), 0, 0, 0, FF_BN)),
            k=D_MODEL, n=D_FF, bm=FF_BM, bn=FF_BN, out_dtypes=(BF16,), epilogue=_ep_silu_mul, name="ffn_gate_up")
        y = _dense(hid, ffn_down_w, (i,), k=D_FF, n=D_MODEL, out_dtype=F32, name="ffn_down",
                   bm=DOWN_BM, bn=DOWN_BN)
        g_next = gain(mix_pre_g, i + 1) if i + 1 < DEPTH else None
        xf, h = _residual(xf, y, gain(ffn_post_g, i), g_next)

    return xf.reshape(BATCH, SEQ, D_MODEL)
```

```python
import functools

import jax
import jax.numpy as jnp
from jax import lax
from jax.experimental import pallas as pl
from jax.experimental.pallas import tpu as pltpu

D_MODEL = 4096
BATCH = 2
SEQ = 4096
DEPTH = 4
TOKENS = BATCH * SEQ
MEM_LEN = 256
EPS = 1e-6
D_A = D_MODEL // 2
D_B = D_MODEL // 2
A_HEADS = 8
A_HEAD_DIM = D_A // A_HEADS
CHUNK = 128
B_GROUPS = 8
B_GROUP_DIM = D_B // B_GROUPS
CONV_WIDTH = 31
C_GROUPS = 4
C_GROUP_DIM = D_MODEL // C_GROUPS
POOL_WINDOWS = (2, 4, 8, 16)
X_HEADS = 4
X_HEAD_DIM = 512
X_DIM = X_HEADS * X_HEAD_DIM
D_FF = -(-8 * D_MODEL // (3 * 256)) * 256

F32 = jnp.float32
BF16 = jnp.bfloat16

V7X_VMEM_BYTES = 64 * 1024 * 1024
V7X_SUBLANES = 8
MIB = 1024 * 1024

MM_BM = 1024
MM_BN = 1024
PAIR_BN = 512
FF_BM = 2048
FF_BN = 256
DOWN_BM = 512
DOWN_BN = 512
ROW_BM = 256
MIX_BM = 512
CONV_HALO = 32
POOL_HALO = 16
ROW_CHUNK = 64
POOL_BC = 512
ATT_BQ = 512


def _nbytes(shape, dtype):
    n = 1
    for s in shape:
        n *= s
    return n * jnp.dtype(dtype).itemsize


def _vmem_limit(pipelined_bytes, scratch_bytes=0, temp_bytes=0):
    need = (2 * pipelined_bytes + scratch_bytes + temp_bytes) * 5 // 4
    return int(min(max(need, 16 * MIB), V7X_VMEM_BYTES - 2 * MIB))


def _sigmoid(x):
    return 1.0 / (1.0 + jnp.exp(-x))


def _silu(x):
    return x * _sigmoid(x)


def _rms(x, g):
    return x * lax.rsqrt(jnp.mean(x * x, axis=-1, keepdims=True) + EPS) * g


def _layer_norm(x, g):
    mu = jnp.mean(x, axis=-1, keepdims=True)
    xc = x - mu
    var = jnp.mean(xc * xc, axis=-1, keepdims=True)
    return xc * lax.rsqrt(var + EPS) * g


def _rmsnorm_kernel(x_ref, g_ref, o_ref):
    o_ref[...] = _rms(x_ref[...].astype(F32), g_ref[...]).astype(o_ref.dtype)


def _rmsnorm(x, g, name):
    rows, d = x.shape
    bm = ROW_BM
    blocks = _nbytes((bm, d), F32) + _nbytes((bm, d), BF16)
    return pl.pallas_call(
        _rmsnorm_kernel,
        out_shape=jax.ShapeDtypeStruct((rows, d), BF16),
        grid=(rows // bm,),
        in_specs=[pl.BlockSpec((bm, d), lambda i: (i, 0)),
                  pl.BlockSpec((1, d), lambda i: (0, 0))],
        out_specs=pl.BlockSpec((bm, d), lambda i: (i, 0)),
        compiler_params=pltpu.CompilerParams(
            dimension_semantics=("parallel",),
            vmem_limit_bytes=_vmem_limit(blocks, 0, _nbytes((bm, d), F32))),
        name=name,
    )(x, g)


def _residual_kernel(x_ref, y_ref, gp_ref, gn_ref, xo_ref, ho_ref):
    xn = x_ref[...] + _rms(y_ref[...], gp_ref[...])
    xo_ref[...] = xn
    ho_ref[...] = _rms(xn, gn_ref[...]).astype(ho_ref.dtype)


def _residual_last_kernel(x_ref, y_ref, gp_ref, xo_ref):
    xo_ref[...] = x_ref[...] + _rms(y_ref[...], gp_ref[...])


def _residual(x, y, g_post, g_next):
    rows, d = x.shape
    bm = ROW_BM
    row_spec = pl.BlockSpec((bm, d), lambda i: (i, 0))
    g_spec = pl.BlockSpec((1, d), lambda i: (0, 0))
    f32_blk = _nbytes((bm, d), F32)
    if g_next is None:
        return pl.pallas_call(
            _residual_last_kernel,
            out_shape=jax.ShapeDtypeStruct((rows, d), F32),
            grid=(rows // bm,),
            in_specs=[row_spec, row_spec, g_spec],
            out_specs=row_spec,
            compiler_params=pltpu.CompilerParams(
                dimension_semantics=("parallel",),
                vmem_limit_bytes=_vmem_limit(3 * f32_blk, 0, f32_blk)),
            name="residual_last",
        )(x, y, g_post), None
    return pl.pallas_call(
        _residual_kernel,
        out_shape=(jax.ShapeDtypeStruct((rows, d), F32), jax.ShapeDtypeStruct((rows, d), BF16)),
        grid=(rows // bm,),
        in_specs=[row_spec, row_spec, g_spec, g_spec],
        out_specs=(row_spec, row_spec),
        compiler_params=pltpu.CompilerParams(
            dimension_semantics=("parallel",),
            vmem_limit_bytes=_vmem_limit(3 * f32_blk + f32_blk // 2, 0, f32_blk)),
        name="residual_norm",
    )(x, y, g_post, g_next)


def _ws_kernel(*refs, n_a, n_arr, n_extra, n_out, wsrc, kc, nm, nj, bn, epilogue):
    a_refs = refs[:n_a]
    w_hbm = refs[n_a:n_a + n_arr]
    extra = refs[n_a + n_arr:n_a + n_arr + n_extra]
    outs = refs[n_a + n_arr + n_extra:n_a + n_arr + n_extra + n_out]
    wbf, stage, sem = refs[n_a + n_arr + n_extra + n_out:]
    n_w = len(wsrc)
    n_stage = stage.shape[0]
    jb = pl.program_id(0)
    i = pl.program_id(1)
    slot = lax.rem(jb, 2)

    def chunk_copy(w, block, c, buf=0):
        arr, lead, row0, col0, row_stride, col_stride = wsrc[w]
        rows = pl.ds(row0 + block * row_stride + c * kc, kc)
        cols = pl.ds(col0 + block * col_stride, bn)
        return pltpu.make_async_copy(w_hbm[arr].at[(*lead, rows, cols)], stage.at[buf, w], sem.at[buf, w])

    def land(w, dst_slot, c, buf=0):
        r0 = c * kc if isinstance(c, int) else pl.multiple_of(c * kc, kc)
        wbf[dst_slot, w, pl.ds(r0, kc), :] = stage[buf, w].astype(BF16)

    def start_all(block, c, buf=0):
        for w in range(n_w):
            chunk_copy(w, block, c, buf).start()

    def finish_all(block, c, dst_slot, buf=0):
        for w in range(n_w):
            chunk_copy(w, block, c, buf).wait()
            land(w, dst_slot, c, buf)

    @pl.when((jb == 0) & (i == 0))
    def _():
        start_all(0, 0, 0)
        for c in range(nm):
            if c + 1 < nm:
                start_all(0, c + 1, (c + 1) % n_stage)
            finish_all(0, c, 0, c % n_stage)
        if nj > 1:
            start_all(1, 0)

    a_off = 0
    accs = [None] * n_w
    for a_ref in a_refs:
        a = a_ref[...]
        ka = a.shape[1]
        for w in range(n_w):
            part = jnp.dot(a, wbf[slot, w, pl.ds(a_off, ka), :], preferred_element_type=F32)
            accs[w] = part if accs[w] is None else accs[w] + part
        a_off += ka
    results = epilogue(accs, extra)
    for o_ref, r in zip(outs, results):
        o_ref[...] = r.astype(o_ref.dtype)

    has_next = jb + 1 < nj

    @pl.when(has_next)
    def _():
        finish_all(jb + 1, i, 1 - slot)

    @pl.when(has_next & (i + 1 < nm))
    def _():
        start_all(jb + 1, i + 1)

    @pl.when((jb + 2 < nj) & (i + 1 == nm))
    def _():
        start_all(jb + 2, 0)


def _ws_matmul(a, weights, wsrc, *, k, n, bm, bn, out_dtypes, epilogue, name,
               extras=(), a_follows_block=False):
    a_parts = a if isinstance(a, (tuple, list)) else (a,)
    m = a_parts[0].shape[0]
    nm = m // bm
    nj = n // bn
    kc = k // nm
    n_w = len(wsrc)
    n_stage = 2 if nm > 1 else 1
    assert m % bm == 0 and n % bn == 0 and k % nm == 0 and kc % (2 * V7X_SUBLANES) == 0
    if a_follows_block:
        assert len(a_parts) == 1 and a_parts[0].shape[1] == k * nj
        a_specs = [pl.BlockSpec((bm, k), lambda jb, i: (i, jb))]
    else:
        assert sum(p.shape[1] for p in a_parts) == k
        a_specs = [pl.BlockSpec((bm, p.shape[1]), lambda jb, i: (i, 0)) for p in a_parts]
    out_spec = pl.BlockSpec((bm, bn), lambda jb, i: (i, jb))
    pipelined = (_nbytes((bm, k), BF16) + sum(_nbytes((bm, bn), d) for d in out_dtypes)
                 + len(extras) * _nbytes((V7X_SUBLANES, bn), F32))
    scratch = 2 * n_w * _nbytes((k, bn), BF16) + n_stage * n_w * _nbytes((kc, bn), F32)
    temps = (n_w + 1) * _nbytes((bm, bn), F32)
    kern = functools.partial(_ws_kernel, n_a=len(a_parts), n_arr=len(weights), n_extra=len(extras),
                             n_out=len(out_dtypes), wsrc=wsrc, kc=kc, nm=nm, nj=nj, bn=bn, epilogue=epilogue)
    return pl.pallas_call(
        kern,
        out_shape=tuple(jax.ShapeDtypeStruct((m, n), d) for d in out_dtypes),
        grid=(nj, nm),
        in_specs=(a_specs + [pl.BlockSpec(memory_space=pl.ANY)] * len(weights)
                  + [pl.BlockSpec((1, bn), lambda jb, i: (0, jb))] * len(extras)),
        out_specs=tuple(out_spec for _ in out_dtypes),
        scratch_shapes=[pltpu.VMEM((2, n_w, k, bn), BF16),
                        pltpu.VMEM((n_stage, n_w, kc, bn), F32),
                        pltpu.SemaphoreType.DMA((n_stage, n_w))],
        compiler_params=pltpu.CompilerParams(
            dimension_semantics=("arbitrary", "arbitrary"),
            vmem_limit_bytes=_vmem_limit(pipelined, scratch, temps)),
        name=name,
    )(*a_parts, *weights, *extras)


def _ep_identity(accs, extra):
    return (accs[0],)


def _ep_silu_mul(accs, extra):
    return (_silu(accs[0]) * accs[1],)


def _ep_glu(accs, extra):
    return (accs[0] * _sigmoid(accs[1]),)


def _ep_pair(accs, extra):
    return (accs[0], accs[1])


def _ep_scale(accs, extra):
    return (accs[0] * extra[0][...],)


def _ep_gating_inputs(accs, extra):
    gv = extra[0][...]
    v = jax.nn.gelu(accs[1])
    heads = []
    for h in range(v.shape[1] // A_HEAD_DIM):
        cols = slice(h * A_HEAD_DIM, (h + 1) * A_HEAD_DIM)
        heads.append(_layer_norm(v[:, cols], gv[:, cols]))
    return (jax.nn.gelu(accs[0]), jnp.concatenate(heads, axis=1))


def _dense(a, w, lead, *, k, n, out_dtype, name, bm=MM_BM, bn=MM_BN):
    return _ws_matmul(a, [w], ((0, lead, 0, 0, 0, bn),), k=k, n=n, bm=bm, bn=bn,
                      out_dtypes=(out_dtype,), epilogue=_ep_identity, name=name)[0]


def _fill_shifted(zs, phases):
    rows = zs.shape[1] - V7X_SUBLANES
    for p in phases:
        if p:
            zs[p, pl.ds(0, rows), :] = zs[0, pl.ds(p, rows), :]


def _read_shifted(zs, row, n_rows):
    p = row % V7X_SUBLANES
    return zs[p, pl.ds(row - p, n_rows), :]


def _a_mixer_kernel(u_ref, v_ref, ws_ref, bs_ref, o_ref):
    t_idx = lax.broadcasted_iota(jnp.int32, (CHUNK, CHUNK), 0)
    s_idx = lax.broadcasted_iota(jnp.int32, (CHUNK, CHUNK), 1)
    causal = s_idx <= t_idx
    for h in range(A_HEADS):
        cols = pl.ds(h * A_HEAD_DIM, A_HEAD_DIM)
        w = jnp.where(causal, ws_ref[h], 0.0).astype(BF16)
        bias = bs_ref[h]
        for c in range(MIX_BM // CHUNK):
            rows = pl.ds(c * CHUNK, CHUNK)
            vm = jnp.dot(w, v_ref[rows, cols], preferred_element_type=F32) + bias
            o_ref[rows, cols] = (u_ref[rows, cols] * vm).astype(o_ref.dtype)


def _a_mixer(ug, vln, w_s, b_s):
    t = ug.shape[0]
    blocks = (_nbytes((MIX_BM, D_A), F32) + 2 * _nbytes((MIX_BM, D_A), BF16)
              + _nbytes((A_HEADS, CHUNK, CHUNK), F32) + _nbytes((A_HEADS, CHUNK, 128), F32))
    row_spec = pl.BlockSpec((MIX_BM, D_A), lambda i: (i, 0))
    return pl.pallas_call(
        _a_mixer_kernel,
        out_shape=jax.ShapeDtypeStruct((t, D_A), BF16),
        grid=(t // MIX_BM,),
        in_specs=[row_spec, row_spec,
                  pl.BlockSpec((A_HEADS, CHUNK, CHUNK), lambda i: (0, 0, 0)),
                  pl.BlockSpec((A_HEADS, CHUNK, 1), lambda i: (0, 0, 0))],
        out_specs=row_spec,
        compiler_params=pltpu.CompilerParams(
            dimension_semantics=("parallel",),
            vmem_limit_bytes=_vmem_limit(blocks)),
        name="gating_mixer",
    )(ug, vln, w_s, b_s.reshape(A_HEADS, CHUNK, 1))


_CONV_TAP0 = CONV_HALO - (CONV_WIDTH - 1)


def _b_mixer_kernel(z_ref, hz_ref, cw_ref, cb_ref, gn_ref, o_ref, zs):
    i = pl.program_id(0)
    at_seq_start = lax.rem(i * MIX_BM, SEQ) == 0
    zs[0, pl.ds(0, CONV_HALO), :] = jnp.where(at_seq_start, 0.0, hz_ref[...])
    zs[0, pl.ds(CONV_HALO, MIX_BM), :] = z_ref[...]
    _fill_shifted(zs, range(V7X_SUBLANES))
    cb = cb_ref[...]
    gn = gn_ref[...]
    for r in range(MIX_BM // ROW_CHUNK):
        r0 = r * ROW_CHUNK
        acc = jnp.zeros((ROW_CHUNK, B_GROUP_DIM), F32)
        for k in range(CONV_WIDTH):
            acc = acc + cw_ref[pl.ds(k, 1), :] * _read_shifted(zs, r0 + _CONV_TAP0 + k, ROW_CHUNK)
        y = _layer_norm(acc + cb, gn)
        o_ref[pl.ds(r0, ROW_CHUNK), :] = _silu(y).astype(o_ref.dtype)


def _b_mixer(z, conv_w, conv_b, gn_g):
    t = z.shape[0]
    gd = B_GROUP_DIM
    halo_per_blk = MIX_BM // CONV_HALO
    blocks = (_nbytes((MIX_BM, gd), F32) + 2 * _nbytes((CONV_HALO, gd), F32) + _nbytes((MIX_BM, gd), BF16))
    scratch = _nbytes((V7X_SUBLANES, MIX_BM + CONV_HALO, gd), F32)
    return pl.pallas_call(
        _b_mixer_kernel,
        out_shape=jax.ShapeDtypeStruct((t, D_B), BF16),
        grid=(t // MIX_BM, B_GROUPS),
        in_specs=[pl.BlockSpec((MIX_BM, gd), lambda i, g: (i, g)),
                  pl.BlockSpec((CONV_HALO, gd), lambda i, g: (jnp.maximum(i * halo_per_blk - 1, 0), g)),
                  pl.BlockSpec((CONV_WIDTH, gd), lambda i, g: (0, g)),
                  pl.BlockSpec((1, gd), lambda i, g: (0, g)),
                  pl.BlockSpec((1, gd), lambda i, g: (0, g))],
        out_specs=pl.BlockSpec((MIX_BM, gd), lambda i, g: (i, g)),
        scratch_shapes=[pltpu.VMEM((V7X_SUBLANES, MIX_BM + CONV_HALO, gd), F32)],
        compiler_params=pltpu.CompilerParams(
            dimension_semantics=("parallel", "parallel"),
            vmem_limit_bytes=_vmem_limit(blocks, scratch)),
        name="conv_mixer",
    )(z, z, conv_w, conv_b.reshape(1, D_B), gn_g.reshape(1, D_B))


def _pool_kernel(p_ref, h_ref, o_ref, zs):
    i = pl.program_id(0)
    g = pl.program_id(1)
    t0 = lax.rem(i * MIX_BM, SEQ)
    zs[0, pl.ds(0, POOL_HALO), :] = jnp.where(t0 == 0, 0.0, h_ref[...])
    zs[0, pl.ds(POOL_HALO, MIX_BM), :] = p_ref[...]

    for gi, win in enumerate(POOL_WINDOWS):

        @pl.when(g == gi)
        def _(win=win):
            _fill_shifted(zs, sorted({(POOL_HALO - j) % V7X_SUBLANES for j in range(win)}))
            for r in range(MIX_BM // ROW_CHUNK):
                r0 = r * ROW_CHUNK
                cur = zs[0, pl.ds(POOL_HALO + r0, ROW_CHUNK), :]
                s = cur
                for j in range(1, win):
                    s = s + _read_shifted(zs, POOL_HALO + r0 - j, ROW_CHUNK)
                t = t0 + r0 + lax.broadcasted_iota(jnp.int32, (ROW_CHUNK, 1), 0)
                count = jnp.minimum(t + 1, win).astype(F32)
                o_ref[pl.ds(r0, ROW_CHUNK), :] = (s / count - cur).astype(o_ref.dtype)


def _pool(p):
    t, d = p.shape
    cols_per_group = C_GROUP_DIM // POOL_BC
    halo_per_blk = MIX_BM // POOL_HALO
    blocks = (_nbytes((MIX_BM, POOL_BC), F32) + _nbytes((POOL_HALO, POOL_BC), F32)
              + _nbytes((MIX_BM, POOL_BC), BF16))
    scratch = _nbytes((V7X_SUBLANES, MIX_BM + POOL_HALO, POOL_BC), F32)
    return pl.pallas_call(
        _pool_kernel,
        out_shape=jax.ShapeDtypeStruct((t, d), BF16),
        grid=(t // MIX_BM, C_GROUPS, cols_per_group),
        in_specs=[pl.BlockSpec((MIX_BM, POOL_BC), lambda i, g, c: (i, g * cols_per_group + c)),
                  pl.BlockSpec((POOL_HALO, POOL_BC),
                               lambda i, g, c: (jnp.maximum(i * halo_per_blk - 1, 0),
                                                g * cols_per_group + c))],
        out_specs=pl.BlockSpec((MIX_BM, POOL_BC), lambda i, g, c: (i, g * cols_per_group + c)),
        scratch_shapes=[pltpu.VMEM((V7X_SUBLANES, MIX_BM + POOL_HALO, POOL_BC), F32)],
        compiler_params=pltpu.CompilerParams(
            dimension_semantics=("parallel", "parallel", "parallel"),
            vmem_limit_bytes=_vmem_limit(blocks, scratch)),
        name="pool_mixer",
    )(p, p)


def _attn_kernel(q_ref, k_ref, v_ref, o_ref):
    s = lax.dot_general(q_ref[...], k_ref[...], (((1,), (1,)), ((), ())),
                        preferred_element_type=F32) * (X_HEAD_DIM ** -0.5)
    e = jnp.exp(s - jnp.max(s, axis=-1, keepdims=True))
    p = e / jnp.sum(e, axis=-1, keepdims=True)
    o_ref[...] = jnp.dot(p.astype(BF16), v_ref[...], preferred_element_type=F32).astype(o_ref.dtype)


def _attention(q, k, v):
    t = q.shape[0]
    q_blocks_per_batch = SEQ // ATT_BQ
    hd = X_HEAD_DIM
    blocks = 2 * _nbytes((ATT_BQ, hd), BF16) + 2 * _nbytes((MEM_LEN, hd), BF16)
    kv_spec = pl.BlockSpec((MEM_LEN, hd), lambda i, h: (i // q_blocks_per_batch, h))
    return pl.pallas_call(
        _attn_kernel,
        out_shape=jax.ShapeDtypeStruct((t, X_DIM), BF16),
        grid=(t // ATT_BQ, X_HEADS),
        in_specs=[pl.BlockSpec((ATT_BQ, hd), lambda i, h: (i, h)), kv_spec, kv_spec],
        out_specs=pl.BlockSpec((ATT_BQ, hd), lambda i, h: (i, h)),
        compiler_params=pltpu.CompilerParams(
            dimension_semantics=("parallel", "parallel"),
            vmem_limit_bytes=_vmem_limit(blocks, 0, 4 * _nbytes((ATT_BQ, MEM_LEN), F32))),
        name="mem_attention",
    )(q, k, v)


def kernel(x, mem, mix_pre_g, mix_post_g, xattn_pre_g, mem_norm_g, xattn_post_g, ffn_pre_g, ffn_post_g, ab_w_in, a_w_s, a_b_s, a_v_norm_g, b_conv_w, b_conv_b, b_gn_g, ab_w_out, c_w_in, c_group_w, c_scale, c_w_out, xq_w, xk_w, xv_w, xo_w, ffn_gate_w, ffn_up_w, ffn_down_w):
    def gain(g, i):
        return g[i].reshape(1, D_MODEL)

    xf = x.reshape(TOKENS, D_MODEL)
    memf = mem.reshape(BATCH * MEM_LEN, D_MODEL)
    group_w = c_group_w.reshape(DEPTH // 2, D_MODEL, C_GROUP_DIM)
    h = _rmsnorm(xf, gain(mix_pre_g, 0), "pre_norm")

    for i in range(DEPTH):
        j = i // 2
        if i % 2 == 0:
            bn = PAIR_BN
            ug, vln = _ws_matmul(
                h, [ab_w_in], ((0, (j,), 0, 0, 0, bn), (0, (j,), 0, D_A, 0, bn)),
                k=D_MODEL, n=D_A, bm=MM_BM, bn=bn, out_dtypes=(F32, BF16), epilogue=_ep_gating_inputs,
                extras=(a_v_norm_g[j].reshape(1, D_A),), name="mix_in_gating")
            (zb,) = _ws_matmul(
                h, [ab_w_in], ((0, (j,), 0, 2 * D_A, 0, bn), (0, (j,), 0, 2 * D_A + D_B, 0, bn)),
                k=D_MODEL, n=D_B, bm=MM_BM, bn=bn, out_dtypes=(F32,), epilogue=_ep_glu, name="mix_in_glu")
            ya = _a_mixer(ug, vln, a_w_s[j], a_b_s[j])
            yb = _b_mixer(zb, b_conv_w[j], b_conv_b[j], b_gn_g[j])
            y = _dense((ya, yb), ab_w_out, (j,), k=D_A + D_B, n=D_MODEL, out_dtype=F32, name="mix_out")
        else:
            p = _pool(_dense(h, c_w_in, (j,), k=D_MODEL, n=D_MODEL, out_dtype=F32, name="pool_in"))
            (yg,) = _ws_matmul(
                p, [group_w], ((0, (j,), 0, 0, C_GROUP_DIM, 0),),
                k=C_GROUP_DIM, n=D_MODEL, bm=MM_BM, bn=C_GROUP_DIM, out_dtypes=(BF16,), epilogue=_ep_scale,
                extras=(c_scale[j].reshape(1, D_MODEL),), a_follows_block=True, name="pool_group")
            y = _dense(yg, c_w_out, (j,), k=D_MODEL, n=D_MODEL, out_dtype=F32, name="pool_out")
        xf, h = _residual(xf, y, gain(mix_post_g, i), gain(xattn_pre_g, i))

        m = _rmsnorm(memf, gain(mem_norm_g, i), "mem_norm")
        q = _dense(h, xq_w, (i,), k=D_MODEL, n=X_DIM, out_dtype=BF16, name="attn_q")
        kk, vv = _ws_matmul(
            m, [xk_w, xv_w], ((0, (i,), 0, 0, 0, PAIR_BN), (1, (i,), 0, 0, 0, PAIR_BN)),
            k=D_MODEL, n=X_DIM, bm=BATCH * MEM_LEN, bn=PAIR_BN, out_dtypes=(BF16, BF16), epilogue=_ep_pair,
            name="attn_kv")
        o = _attention(q, kk, vv)
        y = _dense(o, xo_w, (i,), k=X_DIM, n=D_MODEL, out_dtype=F32, name="attn_out")
        xf, h = _residual(xf, y, gain(xattn_post_g, i), gain(ffn_pre_g, i))

        (hid,) = _ws_matmul(
            h, [ffn_gate_w, ffn_up_w], ((0, (i,), 0, 0, 0, FF_BN), (1, (i,), 0, 0, 0, FF_BN)),
            k=D_MODEL, n=D_FF, bm=FF_BM, bn=FF_BN, out_dtypes=(BF16,), epilogue=_ep_silu_mul, name="ffn_gate_up")
        y = _dense(hid, ffn_down_w, (i,), k=D_FF, n=D_MODEL, out_dtype=F32, name="ffn_down",
                   bm=DOWN_BM, bn=DOWN_BN)
        g_next = gain(mix_pre_g, i + 1) if i + 1 < DEPTH else None
        xf, h = _residual(xf, y, gain(ffn_post_g, i), g_next)

    return xf.reshape(BATCH, SEQ, D_MODEL)
```

```python
import functools

import jax
import jax.numpy as jnp
from jax import lax
from jax.experimental import pallas as pl
from jax.experimental.pallas import tpu as pltpu

D_MODEL = 4096
BATCH = 2
SEQ = 4096
DEPTH = 4
TOKENS = BATCH * SEQ
MEM_LEN = 256
EPS = 1e-6
D_A = D_MODEL // 2
D_B = D_MODEL // 2
A_HEADS = 8
A_HEAD_DIM = D_A // A_HEADS
CHUNK = 128
B_GROUPS = 8
B_GROUP_DIM = D_B // B_GROUPS
CONV_WIDTH = 31
C_GROUPS = 4
C_GROUP_DIM = D_MODEL // C_GROUPS
POOL_WINDOWS = (2, 4, 8, 16)
X_HEADS = 4
X_HEAD_DIM = 512
X_DIM = X_HEADS * X_HEAD_DIM
D_FF = -(-8 * D_MODEL // (3 * 256)) * 256

F32 = jnp.float32
BF16 = jnp.bfloat16

V7X_VMEM_BYTES = 64 * 1024 * 1024
V7X_SUBLANES = 8
MIB = 1024 * 1024

MM_BM = 1024
MM_BN = 1024
PAIR_BN = 512
FF_BM = 2048
FF_BN = 256
DOWN_BM = 512
DOWN_BN = 512
RES_BM = 128
RES_BM_WIDE = 128
ROW_BM = 256
MIX_BM = 512
CONV_HALO = 32
POOL_HALO = 16
ROW_CHUNK = 64
POOL_BC = 512
ATT_BQ = 512


def _nbytes(shape, dtype):
    n = 1
    for s in shape:
        n *= s
    return n * jnp.dtype(dtype).itemsize


def _vmem_limit(pipelined_bytes, scratch_bytes=0, temp_bytes=0):
    need = (2 * pipelined_bytes + scratch_bytes + temp_bytes) * 5 // 4
    return int(min(max(need, 16 * MIB), V7X_VMEM_BYTES - 2 * MIB))


def _sigmoid(x):
    return 1.0 / (1.0 + jnp.exp(-x))


def _silu(x):
    return x * _sigmoid(x)


def _rms(x, g):
    return x * lax.rsqrt(jnp.mean(x * x, axis=-1, keepdims=True) + EPS) * g


def _layer_norm(x, g):
    mu = jnp.mean(x, axis=-1, keepdims=True)
    xc = x - mu
    var = jnp.mean(xc * xc, axis=-1, keepdims=True)
    return xc * lax.rsqrt(var + EPS) * g


def _rmsnorm_kernel(x_ref, g_ref, o_ref):
    o_ref[...] = _rms(x_ref[...].astype(F32), g_ref[...]).astype(o_ref.dtype)


def _rmsnorm(x, g, name):
    rows, d = x.shape
    bm = ROW_BM
    blocks = _nbytes((bm, d), F32) + _nbytes((bm, d), BF16)
    return pl.pallas_call(
        _rmsnorm_kernel,
        out_shape=jax.ShapeDtypeStruct((rows, d), BF16),
        grid=(rows // bm,),
        in_specs=[pl.BlockSpec((bm, d), lambda i: (i, 0)),
                  pl.BlockSpec((1, d), lambda i: (0, 0))],
        out_specs=pl.BlockSpec((bm, d), lambda i: (i, 0)),
        compiler_params=pltpu.CompilerParams(
            dimension_semantics=("parallel",),
            vmem_limit_bytes=_vmem_limit(blocks, 0, _nbytes((bm, d), F32))),
        name=name,
    )(x, g)


def _residual_kernel(x_ref, y_ref, gp_ref, gn_ref, xo_ref, ho_ref):
    xn = x_ref[...] + _rms(y_ref[...], gp_ref[...])
    xo_ref[...] = xn
    ho_ref[...] = _rms(xn, gn_ref[...]).astype(ho_ref.dtype)


def _residual_last_kernel(x_ref, y_ref, gp_ref, xo_ref):
    xo_ref[...] = x_ref[...] + _rms(y_ref[...], gp_ref[...])


def _residual(x, y, g_post, g_next):
    rows, d = x.shape
    bm = ROW_BM
    row_spec = pl.BlockSpec((bm, d), lambda i: (i, 0))
    g_spec = pl.BlockSpec((1, d), lambda i: (0, 0))
    f32_blk = _nbytes((bm, d), F32)
    if g_next is None:
        return pl.pallas_call(
            _residual_last_kernel,
            out_shape=jax.ShapeDtypeStruct((rows, d), F32),
            grid=(rows // bm,),
            in_specs=[row_spec, row_spec, g_spec],
            out_specs=row_spec,
            compiler_params=pltpu.CompilerParams(
                dimension_semantics=("parallel",),
                vmem_limit_bytes=_vmem_limit(3 * f32_blk, 0, f32_blk)),
            name="residual_last",
        )(x, y, g_post), None
    return pl.pallas_call(
        _residual_kernel,
        out_shape=(jax.ShapeDtypeStruct((rows, d), F32), jax.ShapeDtypeStruct((rows, d), BF16)),
        grid=(rows // bm,),
        in_specs=[row_spec, row_spec, g_spec, g_spec],
        out_specs=(row_spec, row_spec),
        compiler_params=pltpu.CompilerParams(
            dimension_semantics=("parallel",),
            vmem_limit_bytes=_vmem_limit(3 * f32_blk + f32_blk // 2, 0, f32_blk)),
        name="residual_norm",
    )(x, y, g_post, g_next)


def _ws_kernel(*refs, n_a, n_arr, n_extra, n_out, wsrc, kc, nm, nj, bn, epilogue):
    a_refs = refs[:n_a]
    w_hbm = refs[n_a:n_a + n_arr]
    extra = refs[n_a + n_arr:n_a + n_arr + n_extra]
    outs = refs[n_a + n_arr + n_extra:n_a + n_arr + n_extra + n_out]
    wbf, stage, sem = refs[n_a + n_arr + n_extra + n_out:]
    n_w = len(wsrc)
    n_stage = stage.shape[0]
    jb = pl.program_id(0)
    i = pl.program_id(1)
    slot = lax.rem(jb, 2)

    def chunk_copy(w, block, c, buf=0):
        arr, lead, row0, col0, row_stride, col_stride = wsrc[w]
        rows = pl.ds(row0 + block * row_stride + c * kc, kc)
        cols = pl.ds(col0 + block * col_stride, bn)
        return pltpu.make_async_copy(w_hbm[arr].at[(*lead, rows, cols)], stage.at[buf, w], sem.at[buf, w])

    def land(w, dst_slot, c, buf=0):
        r0 = c * kc if isinstance(c, int) else pl.multiple_of(c * kc, kc)
        wbf[dst_slot, w, pl.ds(r0, kc), :] = stage[buf, w].astype(BF16)

    def start_all(block, c, buf=0):
        for w in range(n_w):
            chunk_copy(w, block, c, buf).start()

    def finish_all(block, c, dst_slot, buf=0):
        for w in range(n_w):
            chunk_copy(w, block, c, buf).wait()
            land(w, dst_slot, c, buf)

    @pl.when((jb == 0) & (i == 0))
    def _():
        start_all(0, 0, 0)
        for c in range(nm):
            if c + 1 < nm:
                start_all(0, c + 1, (c + 1) % n_stage)
            finish_all(0, c, 0, c % n_stage)
        if nj > 1:
            start_all(1, 0)

    a_off = 0
    accs = [None] * n_w
    for a_ref in a_refs:
        a = a_ref[...]
        ka = a.shape[1]
        for w in range(n_w):
            part = jnp.dot(a, wbf[slot, w, pl.ds(a_off, ka), :], preferred_element_type=F32)
            accs[w] = part if accs[w] is None else accs[w] + part
        a_off += ka
    results = epilogue(accs, extra)
    for o_ref, r in zip(outs, results):
        o_ref[...] = r.astype(o_ref.dtype)

    has_next = jb + 1 < nj

    @pl.when(has_next)
    def _():
        finish_all(jb + 1, i, 1 - slot)

    @pl.when(has_next & (i + 1 < nm))
    def _():
        start_all(jb + 1, i + 1)

    @pl.when((jb + 2 < nj) & (i + 1 == nm))
    def _():
        start_all(jb + 2, 0)


def _ws_matmul(a, weights, wsrc, *, k, n, bm, bn, out_dtypes, epilogue, name,
               extras=(), a_follows_block=False):
    a_parts = a if isinstance(a, (tuple, list)) else (a,)
    m = a_parts[0].shape[0]
    nm = m // bm
    nj = n // bn
    kc = k // nm
    n_w = len(wsrc)
    n_stage = 2 if nm > 1 else 1
    assert m % bm == 0 and n % bn == 0 and k % nm == 0 and kc % (2 * V7X_SUBLANES) == 0
    if a_follows_block:
        assert len(a_parts) == 1 and a_parts[0].shape[1] == k * nj
        a_specs = [pl.BlockSpec((bm, k), lambda jb, i: (i, jb))]
    else:
        assert sum(p.shape[1] for p in a_parts) == k
        a_specs = [pl.BlockSpec((bm, p.shape[1]), lambda jb, i: (i, 0)) for p in a_parts]
    out_spec = pl.BlockSpec((bm, bn), lambda jb, i: (i, jb))
    pipelined = (_nbytes((bm, k), BF16) + sum(_nbytes((bm, bn), d) for d in out_dtypes)
                 + len(extras) * _nbytes((V7X_SUBLANES, bn), F32))
    scratch = 2 * n_w * _nbytes((k, bn), BF16) + n_stage * n_w * _nbytes((kc, bn), F32)
    temps = (n_w + 1) * _nbytes((bm, bn), F32)
    kern = functools.partial(_ws_kernel, n_a=len(a_parts), n_arr=len(weights), n_extra=len(extras),
                             n_out=len(out_dtypes), wsrc=wsrc, kc=kc, nm=nm, nj=nj, bn=bn, epilogue=epilogue)
    return pl.pallas_call(
        kern,
        out_shape=tuple(jax.ShapeDtypeStruct((m, n), d) for d in out_dtypes),
        grid=(nj, nm),
        in_specs=(a_specs + [pl.BlockSpec(memory_space=pl.ANY)] * len(weights)
                  + [pl.BlockSpec((1, bn), lambda jb, i: (0, jb))] * len(extras)),
        out_specs=tuple(out_spec for _ in out_dtypes),
        scratch_shapes=[pltpu.VMEM((2, n_w, k, bn), BF16),
                        pltpu.VMEM((n_stage, n_w, kc, bn), F32),
                        pltpu.SemaphoreType.DMA((n_stage, n_w))],
        compiler_params=pltpu.CompilerParams(
            dimension_semantics=("arbitrary", "arbitrary"),
            vmem_limit_bytes=_vmem_limit(pipelined, scratch, temps)),
        name=name,
    )(*a_parts, *weights, *extras)


def _ep_identity(accs, extra):
    return (accs[0],)


def _ep_silu_mul(accs, extra):
    return (_silu(accs[0]) * accs[1],)


def _ep_glu(accs, extra):
    return (accs[0] * _sigmoid(accs[1]),)


def _ep_pair(accs, extra):
    return (accs[0], accs[1])


def _ep_scale(accs, extra):
    return (accs[0] * extra[0][...],)


def _ep_gating_inputs(accs, extra):
    gv = extra[0][...]
    v = jax.nn.gelu(accs[1])
    heads = []
    for h in range(v.shape[1] // A_HEAD_DIM):
        cols = slice(h * A_HEAD_DIM, (h + 1) * A_HEAD_DIM)
        heads.append(_layer_norm(v[:, cols], gv[:, cols]))
    return (jax.nn.gelu(accs[0]), jnp.concatenate(heads, axis=1))


def _dense(a, w, lead, *, k, n, out_dtype, name, bm=MM_BM, bn=MM_BN):
    return _ws_matmul(a, [w], ((0, lead, 0, 0, 0, bn),), k=k, n=n, bm=bm, bn=bn,
                      out_dtypes=(out_dtype,), epilogue=_ep_identity, name=name)[0]


RES_KC = 128


def _proj_residual_kernel(*refs, n_a, lead, k):
    a_refs = refs[:n_a]
    w_hbm, x_ref, gp_ref, gn_ref, xo_ref, ho_ref, wbf, y_scr, stage, sem = refs[n_a:]
    n_chunks = k // RES_KC
    i = pl.program_id(0)
    slot = lax.rem(i, 2)

    def chunk_copy(c):
        return pltpu.make_async_copy(w_hbm.at[(*lead, pl.ds(c * RES_KC, RES_KC), slice(None))],
                                     stage.at[c % 2], sem.at[c % 2])

    @pl.when(i == 0)
    def _():
        y_scr[1] = jnp.zeros(y_scr.shape[1:], F32)
        chunk_copy(0).start()
        for c in range(n_chunks):
            if c + 1 < n_chunks:
                chunk_copy(c + 1).start()
            chunk_copy(c).wait()
            wbf[pl.ds(c * RES_KC, RES_KC), :] = stage[c % 2].astype(BF16)

    def body(cur, prev):
        xn = x_ref[...] + _rms(y_scr[prev], gp_ref[...])
        xo_ref[...] = xn
        ho_ref[...] = _rms(xn, gn_ref[...]).astype(ho_ref.dtype)
        a_off = 0
        y = None
        for a_ref in a_refs:
            ka = a_ref.shape[1]
            part = jnp.dot(a_ref[...], wbf[pl.ds(a_off, ka), :], preferred_element_type=F32)
            y = part if y is None else y + part
            a_off += ka
        y_scr[cur] = y

    @pl.when(slot == 0)
    def _():
        body(0, 1)

    @pl.when(slot == 1)
    def _():
        body(1, 0)


def _proj_residual(a, w, lead, x, g_post, g_next, *, k, bm, name):
    a_parts = a if isinstance(a, (tuple, list)) else (a,)
    m, d = x.shape
    assert sum(p.shape[1] for p in a_parts) == k and k % RES_KC == 0 and m % bm == 0
    n_blocks = m // bm
    row_spec = pl.BlockSpec((bm, d), lambda i: (jnp.maximum(i - 1, 0), 0))
    g_spec = pl.BlockSpec((1, d), lambda i: (0, 0))
    pipelined = _nbytes((bm, k), BF16) + 2 * _nbytes((bm, d), F32) + _nbytes((bm, d), BF16)
    scratch = _nbytes((k, d), BF16) + 2 * _nbytes((RES_KC, d), F32) + 2 * _nbytes((bm, d), F32)
    temps = 2 * _nbytes((bm, d), F32)
    kern = functools.partial(_proj_residual_kernel, n_a=len(a_parts), lead=lead, k=k)
    return pl.pallas_call(
        kern,
        out_shape=(jax.ShapeDtypeStruct((m, d), F32), jax.ShapeDtypeStruct((m, d), BF16)),
        grid=(n_blocks + 1,),
        in_specs=([pl.BlockSpec((bm, p.shape[1]), lambda i: (jnp.minimum(i, n_blocks - 1), 0))
                   for p in a_parts]
                  + [pl.BlockSpec(memory_space=pl.ANY), row_spec, g_spec, g_spec]),
        out_specs=(row_spec, row_spec),
        scratch_shapes=[pltpu.VMEM((k, d), BF16), pltpu.VMEM((2, bm, d), F32),
                        pltpu.VMEM((2, RES_KC, d), F32), pltpu.SemaphoreType.DMA((2,))],
        compiler_params=pltpu.CompilerParams(
            dimension_semantics=("arbitrary",),
            vmem_limit_bytes=_vmem_limit(pipelined, scratch, temps)),
        name=name,
    )(*a_parts, w, x, g_post, g_next)


def _fill_shifted(zs, phases):
    rows = zs.shape[1] - V7X_SUBLANES
    for p in phases:
        if p:
            zs[p, pl.ds(0, rows), :] = zs[0, pl.ds(p, rows), :]


def _read_shifted(zs, row, n_rows):
    p = row % V7X_SUBLANES
    return zs[p, pl.ds(row - p, n_rows), :]


def _a_mixer_kernel(u_ref, v_ref, ws_ref, bs_ref, o_ref):
    t_idx = lax.broadcasted_iota(jnp.int32, (CHUNK, CHUNK), 0)
    s_idx = lax.broadcasted_iota(jnp.int32, (CHUNK, CHUNK), 1)
    causal = s_idx <= t_idx
    for h in range(A_HEADS):
        cols = pl.ds(h * A_HEAD_DIM, A_HEAD_DIM)
        w = jnp.where(causal, ws_ref[h], 0.0).astype(BF16)
        bias = bs_ref[h]
        for c in range(MIX_BM // CHUNK):
            rows = pl.ds(c * CHUNK, CHUNK)
            vm = jnp.dot(w, v_ref[rows, cols], preferred_element_type=F32) + bias
            o_ref[rows, cols] = (u_ref[rows, cols] * vm).astype(o_ref.dtype)


def _a_mixer(ug, vln, w_s, b_s):
    t = ug.shape[0]
    blocks = (_nbytes((MIX_BM, D_A), F32) + 2 * _nbytes((MIX_BM, D_A), BF16)
              + _nbytes((A_HEADS, CHUNK, CHUNK), F32) + _nbytes((A_HEADS, CHUNK, 128), F32))
    row_spec = pl.BlockSpec((MIX_BM, D_A), lambda i: (i, 0))
    return pl.pallas_call(
        _a_mixer_kernel,
        out_shape=jax.ShapeDtypeStruct((t, D_A), BF16),
        grid=(t // MIX_BM,),
        in_specs=[row_spec, row_spec,
                  pl.BlockSpec((A_HEADS, CHUNK, CHUNK), lambda i: (0, 0, 0)),
                  pl.BlockSpec((A_HEADS, CHUNK, 1), lambda i: (0, 0, 0))],
        out_specs=row_spec,
        compiler_params=pltpu.CompilerParams(
            dimension_semantics=("parallel",),
            vmem_limit_bytes=_vmem_limit(blocks)),
        name="gating_mixer",
    )(ug, vln, w_s, b_s.reshape(A_HEADS, CHUNK, 1))


_CONV_TAP0 = CONV_HALO - (CONV_WIDTH - 1)


def _b_mixer_kernel(z_ref, hz_ref, cw_ref, cb_ref, gn_ref, o_ref, zs):
    i = pl.program_id(0)
    at_seq_start = lax.rem(i * MIX_BM, SEQ) == 0
    zs[0, pl.ds(0, CONV_HALO), :] = jnp.where(at_seq_start, 0.0, hz_ref[...])
    zs[0, pl.ds(CONV_HALO, MIX_BM), :] = z_ref[...]
    _fill_shifted(zs, range(V7X_SUBLANES))
    cb = cb_ref[...]
    gn = gn_ref[...]
    for r in range(MIX_BM // ROW_CHUNK):
        r0 = r * ROW_CHUNK
        acc = jnp.zeros((ROW_CHUNK, B_GROUP_DIM), F32)
        for k in range(CONV_WIDTH):
            acc = acc + cw_ref[pl.ds(k, 1), :] * _read_shifted(zs, r0 + _CONV_TAP0 + k, ROW_CHUNK)
        y = _layer_norm(acc + cb, gn)
        o_ref[pl.ds(r0, ROW_CHUNK), :] = _silu(y).astype(o_ref.dtype)


def _b_mixer(z, conv_w, conv_b, gn_g):
    t = z.shape[0]
    gd = B_GROUP_DIM
    halo_per_blk = MIX_BM // CONV_HALO
    blocks = (_nbytes((MIX_BM, gd), F32) + 2 * _nbytes((CONV_HALO, gd), F32) + _nbytes((MIX_BM, gd), BF16))
    scratch = _nbytes((V7X_SUBLANES, MIX_BM + CONV_HALO, gd), F32)
    return pl.pallas_call(
        _b_mixer_kernel,
        out_shape=jax.ShapeDtypeStruct((t, D_B), BF16),
        grid=(t // MIX_BM, B_GROUPS),
        in_specs=[pl.BlockSpec((MIX_BM, gd), lambda i, g: (i, g)),
                  pl.BlockSpec((CONV_HALO, gd), lambda i, g: (jnp.maximum(i * halo_per_blk - 1, 0), g)),
                  pl.BlockSpec((CONV_WIDTH, gd), lambda i, g: (0, g)),
                  pl.BlockSpec((1, gd), lambda i, g: (0, g)),
                  pl.BlockSpec((1, gd), lambda i, g: (0, g))],
        out_specs=pl.BlockSpec((MIX_BM, gd), lambda i, g: (i, g)),
        scratch_shapes=[pltpu.VMEM((V7X_SUBLANES, MIX_BM + CONV_HALO, gd), F32)],
        compiler_params=pltpu.CompilerParams(
            dimension_semantics=("parallel", "parallel"),
            vmem_limit_bytes=_vmem_limit(blocks, scratch)),
        name="conv_mixer",
    )(z, z, conv_w, conv_b.reshape(1, D_B), gn_g.reshape(1, D_B))


def _pool_kernel(p_ref, h_ref, o_ref, zs):
    i = pl.program_id(0)
    g = pl.program_id(1)
    t0 = lax.rem(i * MIX_BM, SEQ)
    zs[0, pl.ds(0, POOL_HALO), :] = jnp.where(t0 == 0, 0.0, h_ref[...])
    zs[0, pl.ds(POOL_HALO, MIX_BM), :] = p_ref[...]

    for gi, win in enumerate(POOL_WINDOWS):

        @pl.when(g == gi)
        def _(win=win):
            _fill_shifted(zs, sorted({(POOL_HALO - j) % V7X_SUBLANES for j in range(win)}))
            for r in range(MIX_BM // ROW_CHUNK):
                r0 = r * ROW_CHUNK
                cur = zs[0, pl.ds(POOL_HALO + r0, ROW_CHUNK), :]
                s = cur
                for j in range(1, win):
                    s = s + _read_shifted(zs, POOL_HALO + r0 - j, ROW_CHUNK)
                t = t0 + r0 + lax.broadcasted_iota(jnp.int32, (ROW_CHUNK, 1), 0)
                count = jnp.minimum(t + 1, win).astype(F32)
                o_ref[pl.ds(r0, ROW_CHUNK), :] = (s / count - cur).astype(o_ref.dtype)


def _pool(p):
    t, d = p.shape
    cols_per_group = C_GROUP_DIM // POOL_BC
    halo_per_blk = MIX_BM // POOL_HALO
    blocks = (_nbytes((MIX_BM, POOL_BC), F32) + _nbytes((POOL_HALO, POOL_BC), F32)
              + _nbytes((MIX_BM, POOL_BC), BF16))
    scratch = _nbytes((V7X_SUBLANES, MIX_BM + POOL_HALO, POOL_BC), F32)
    return pl.pallas_call(
        _pool_kernel,
        out_shape=jax.ShapeDtypeStruct((t, d), BF16),
        grid=(t // MIX_BM, C_GROUPS, cols_per_group),
        in_specs=[pl.BlockSpec((MIX_BM, POOL_BC), lambda i, g, c: (i, g * cols_per_group + c)),
                  pl.BlockSpec((POOL_HALO, POOL_BC),
                               lambda i, g, c: (jnp.maximum(i * halo_per_blk - 1, 0),
                                                g * cols_per_group + c))],
        out_specs=pl.BlockSpec((MIX_BM, POOL_BC), lambda i, g, c: (i, g * cols_per_group + c)),
        scratch_shapes=[pltpu.VMEM((V7X_SUBLANES, MIX_BM + POOL_HALO, POOL_BC), F32)],
        compiler_params=pltpu.CompilerParams(
            dimension_semantics=("parallel", "parallel", "parallel"),
            vmem_limit_bytes=_vmem_limit(blocks, scratch)),
        name="pool_mixer",
    )(p, p)


def _attn_kernel(q_ref, k_ref, v_ref, o_ref):
    s = lax.dot_general(q_ref[...], k_ref[...], (((1,), (1,)), ((), ())),
                        preferred_element_type=F32) * (X_HEAD_DIM ** -0.5)
    e = jnp.exp(s - jnp.max(s, axis=-1, keepdims=True))
    p = e / jnp.sum(e, axis=-1, keepdims=True)
    o_ref[...] = jnp.dot(p.astype(BF16), v_ref[...], preferred_element_type=F32).astype(o_ref.dtype)


def _attention(q, k, v):
    t = q.shape[0]
    q_blocks_per_batch = SEQ // ATT_BQ
    hd = X_HEAD_DIM
    blocks = 2 * _nbytes((ATT_BQ, hd), BF16) + 2 * _nbytes((MEM_LEN, hd), BF16)
    kv_spec = pl.BlockSpec((MEM_LEN, hd), lambda i, h: (i // q_blocks_per_batch, h))
    return pl.pallas_call(
        _attn_kernel,
        out_shape=jax.ShapeDtypeStruct((t, X_DIM), BF16),
        grid=(t // ATT_BQ, X_HEADS),
        in_specs=[pl.BlockSpec((ATT_BQ, hd), lambda i, h: (i, h)), kv_spec, kv_spec],
        out_specs=pl.BlockSpec((ATT_BQ, hd), lambda i, h: (i, h)),
        compiler_params=pltpu.CompilerParams(
            dimension_semantics=("parallel", "parallel"),
            vmem_limit_bytes=_vmem_limit(blocks, 0, 4 * _nbytes((ATT_BQ, MEM_LEN), F32))),
        name="mem_attention",
    )(q, k, v)


def kernel(x, mem, mix_pre_g, mix_post_g, xattn_pre_g, mem_norm_g, xattn_post_g, ffn_pre_g, ffn_post_g, ab_w_in, a_w_s, a_b_s, a_v_norm_g, b_conv_w, b_conv_b, b_gn_g, ab_w_out, c_w_in, c_group_w, c_scale, c_w_out, xq_w, xk_w, xv_w, xo_w, ffn_gate_w, ffn_up_w, ffn_down_w):
    def gain(g, i):
        return g[i].reshape(1, D_MODEL)

    xf = x.reshape(TOKENS, D_MODEL)
    memf = mem.reshape(BATCH * MEM_LEN, D_MODEL)
    group_w = c_group_w.reshape(DEPTH // 2, D_MODEL, C_GROUP_DIM)
    h = _rmsnorm(xf, gain(mix_pre_g, 0), "pre_norm")

    for i in range(DEPTH):
        j = i // 2
        if i % 2 == 0:
            bn = PAIR_BN
            ug, vln = _ws_matmul(
                h, [ab_w_in], ((0, (j,), 0, 0, 0, bn), (0, (j,), 0, D_A, 0, bn)),
                k=D_MODEL, n=D_A, bm=MM_BM, bn=bn, out_dtypes=(F32, BF16), epilogue=_ep_gating_inputs,
                extras=(a_v_norm_g[j].reshape(1, D_A),), name="mix_in_gating")
            (zb,) = _ws_matmul(
                h, [ab_w_in], ((0, (j,), 0, 2 * D_A, 0, bn), (0, (j,), 0, 2 * D_A + D_B, 0, bn)),
                k=D_MODEL, n=D_B, bm=MM_BM, bn=bn, out_dtypes=(F32,), epilogue=_ep_glu, name="mix_in_glu")
            ya = _a_mixer(ug, vln, a_w_s[j], a_b_s[j])
            yb = _b_mixer(zb, b_conv_w[j], b_conv_b[j], b_gn_g[j])
            xf, h = _proj_residual((ya, yb), ab_w_out, (j,), xf, gain(mix_post_g, i), gain(xattn_pre_g, i),
                                   k=D_A + D_B, bm=RES_BM_WIDE, name="mix_out_residual")
        else:
            p = _pool(_dense(h, c_w_in, (j,), k=D_MODEL, n=D_MODEL, out_dtype=F32, name="pool_in"))
            (yg,) = _ws_matmul(
                p, [group_w], ((0, (j,), 0, 0, C_GROUP_DIM, 0),),
                k=C_GROUP_DIM, n=D_MODEL, bm=MM_BM, bn=C_GROUP_DIM, out_dtypes=(BF16,), epilogue=_ep_scale,
                extras=(c_scale[j].reshape(1, D_MODEL),), a_follows_block=True, name="pool_group")
            xf, h = _proj_residual(yg, c_w_out, (j,), xf, gain(mix_post_g, i), gain(xattn_pre_g, i),
                                   k=D_MODEL, bm=RES_BM_WIDE, name="pool_out_residual")

        m = _rmsnorm(memf, gain(mem_norm_g, i), "mem_norm")
        q = _dense(h, xq_w, (i,), k=D_MODEL, n=X_DIM, out_dtype=BF16, name="attn_q")
        kk, vv = _ws_matmul(
            m, [xk_w, xv_w], ((0, (i,), 0, 0, 0, PAIR_BN), (1, (i,), 0, 0, 0, PAIR_BN)),
            k=D_MODEL, n=X_DIM, bm=BATCH * MEM_LEN, bn=PAIR_BN, out_dtypes=(BF16, BF16), epilogue=_ep_pair,
            name="attn_kv")
        o = _attention(q, kk, vv)
        xf, h = _proj_residual(o, xo_w, (i,), xf, gain(xattn_post_g, i), gain(ffn_pre_g, i),
                               k=X_DIM, bm=RES_BM, name="attn_out_residual")

        (hid,) = _ws_matmul(
            h, [ffn_gate_w, ffn_up_w], ((0, (i,), 0, 0, 0, FF_BN), (1, (i,), 0, 0, 0, FF_BN)),
            k=D_MODEL, n=D_FF, bm=FF_BM, bn=FF_BN, out_dtypes=(BF16,), epilogue=_ep_silu_mul, name="ffn_gate_up")
        y = _dense(hid, ffn_down_w, (i,), k=D_FF, n=D_MODEL, out_dtype=F32, name="ffn_down",
                   bm=DOWN_BM, bn=DOWN_BN)
        g_next = gain(mix_pre_g, i + 1) if i + 1 < DEPTH else None
        xf, h = _residual(xf, y, gain(ffn_post_g, i), g_next)

    return xf.reshape(BATCH, SEQ, D_MODEL)
```

```python
import functools

import jax
import jax.numpy as jnp
from jax import lax
from jax.experimental import pallas as pl
from jax.experimental.pallas import tpu as pltpu

D_MODEL = 4096
BATCH = 2
SEQ = 4096
DEPTH = 4
TOKENS = BATCH * SEQ
MEM_LEN = 256
EPS = 1e-6
D_A = D_MODEL // 2
D_B = D_MODEL // 2
A_HEADS = 8
A_HEAD_DIM = D_A // A_HEADS
CHUNK = 128
B_GROUPS = 8
B_GROUP_DIM = D_B // B_GROUPS
CONV_WIDTH = 31
C_GROUPS = 4
C_GROUP_DIM = D_MODEL // C_GROUPS
POOL_WINDOWS = (2, 4, 8, 16)
X_HEADS = 4
X_HEAD_DIM = 512
X_DIM = X_HEADS * X_HEAD_DIM
D_FF = -(-8 * D_MODEL // (3 * 256)) * 256

F32 = jnp.float32
BF16 = jnp.bfloat16

V7X_VMEM_BYTES = 64 * 1024 * 1024
V7X_SUBLANES = 8
MIB = 1024 * 1024

MM_BM = 1024
MM_BN = 1024
PAIR_BN = 512
FF_BM = 2048
FF_BN = 256
DOWN_BM = 512
DOWN_BN = 512
RES_BM = 128
RES_BM_WIDE = 128
ROW_BM = 256
MIX_BM = 512
CONV_HALO = 32
POOL_HALO = 16
ROW_CHUNK = 64
POOL_BC = 512
ATT_BQ = 512


def _nbytes(shape, dtype):
    n = 1
    for s in shape:
        n *= s
    return n * jnp.dtype(dtype).itemsize


def _vmem_limit(pipelined_bytes, scratch_bytes=0, temp_bytes=0):
    need = (2 * pipelined_bytes + scratch_bytes + temp_bytes) * 5 // 4
    return int(min(max(need, 16 * MIB), V7X_VMEM_BYTES - 2 * MIB))


def _sigmoid(x):
    return 1.0 / (1.0 + jnp.exp(-x))


def _silu(x):
    return x * _sigmoid(x)


def _rms(x, g):
    return x * lax.rsqrt(jnp.mean(x * x, axis=-1, keepdims=True) + EPS) * g


def _layer_norm(x, g):
    mu = jnp.mean(x, axis=-1, keepdims=True)
    xc = x - mu
    var = jnp.mean(xc * xc, axis=-1, keepdims=True)
    return xc * lax.rsqrt(var + EPS) * g


def _rmsnorm_kernel(x_ref, g_ref, o_ref):
    o_ref[...] = _rms(x_ref[...].astype(F32), g_ref[...]).astype(o_ref.dtype)


def _rmsnorm(x, g, name):
    rows, d = x.shape
    bm = ROW_BM
    blocks = _nbytes((bm, d), F32) + _nbytes((bm, d), BF16)
    return pl.pallas_call(
        _rmsnorm_kernel,
        out_shape=jax.ShapeDtypeStruct((rows, d), BF16),
        grid=(rows // bm,),
        in_specs=[pl.BlockSpec((bm, d), lambda i: (i, 0)),
                  pl.BlockSpec((1, d), lambda i: (0, 0))],
        out_specs=pl.BlockSpec((bm, d), lambda i: (i, 0)),
        compiler_params=pltpu.CompilerParams(
            dimension_semantics=("parallel",),
            vmem_limit_bytes=_vmem_limit(blocks, 0, _nbytes((bm, d), F32))),
        name=name,
    )(x, g)


def _residual_kernel(x_ref, y_ref, gp_ref, gn_ref, xo_ref, ho_ref):
    xn = x_ref[...] + _rms(y_ref[...], gp_ref[...])
    xo_ref[...] = xn
    ho_ref[...] = _rms(xn, gn_ref[...]).astype(ho_ref.dtype)


def _residual_last_kernel(x_ref, y_ref, gp_ref, xo_ref):
    xo_ref[...] = x_ref[...] + _rms(y_ref[...], gp_ref[...])


def _residual(x, y, g_post, g_next):
    rows, d = x.shape
    bm = ROW_BM
    row_spec = pl.BlockSpec((bm, d), lambda i: (i, 0))
    g_spec = pl.BlockSpec((1, d), lambda i: (0, 0))
    f32_blk = _nbytes((bm, d), F32)
    if g_next is None:
        return pl.pallas_call(
            _residual_last_kernel,
            out_shape=jax.ShapeDtypeStruct((rows, d), F32),
            grid=(rows // bm,),
            in_specs=[row_spec, row_spec, g_spec],
            out_specs=row_spec,
            compiler_params=pltpu.CompilerParams(
                dimension_semantics=("parallel",),
                vmem_limit_bytes=_vmem_limit(3 * f32_blk, 0, f32_blk)),
            name="residual_last",
        )(x, y, g_post), None
    return pl.pallas_call(
        _residual_kernel,
        out_shape=(jax.ShapeDtypeStruct((rows, d), F32), jax.ShapeDtypeStruct((rows, d), BF16)),
        grid=(rows // bm,),
        in_specs=[row_spec, row_spec, g_spec, g_spec],
        out_specs=(row_spec, row_spec),
        compiler_params=pltpu.CompilerParams(
            dimension_semantics=("parallel",),
            vmem_limit_bytes=_vmem_limit(3 * f32_blk + f32_blk // 2, 0, f32_blk)),
        name="residual_norm",
    )(x, y, g_post, g_next)


def _ws_kernel(*refs, n_a, n_arr, n_extra, n_out, wsrc, kc, nm, nj, bn, epilogue):
    a_refs = refs[:n_a]
    w_hbm = refs[n_a:n_a + n_arr]
    extra = refs[n_a + n_arr:n_a + n_arr + n_extra]
    outs = refs[n_a + n_arr + n_extra:n_a + n_arr + n_extra + n_out]
    wbf, stage, sem = refs[n_a + n_arr + n_extra + n_out:]
    n_w = len(wsrc)
    n_stage = stage.shape[0]
    jb = pl.program_id(0)
    i = pl.program_id(1)
    slot = lax.rem(jb, 2)

    def chunk_copy(w, block, c, buf=0):
        arr, lead, row0, col0, row_stride, col_stride = wsrc[w]
        rows = pl.ds(row0 + block * row_stride + c * kc, kc)
        cols = pl.ds(col0 + block * col_stride, bn)
        return pltpu.make_async_copy(w_hbm[arr].at[(*lead, rows, cols)], stage.at[buf, w], sem.at[buf, w])

    def land(w, dst_slot, c, buf=0):
        r0 = c * kc if isinstance(c, int) else pl.multiple_of(c * kc, kc)
        wbf[dst_slot, pl.ds(r0, kc), pl.ds(w * bn, bn)] = stage[buf, w].astype(BF16)

    def start_all(block, c, buf=0):
        for w in range(n_w):
            chunk_copy(w, block, c, buf).start()

    def finish_all(block, c, dst_slot, buf=0):
        for w in range(n_w):
            chunk_copy(w, block, c, buf).wait()
            land(w, dst_slot, c, buf)

    @pl.when((jb == 0) & (i == 0))
    def _():
        start_all(0, 0, 0)
        for c in range(nm):
            if c + 1 < nm:
                start_all(0, c + 1, (c + 1) % n_stage)
            finish_all(0, c, 0, c % n_stage)
        if nj > 1:
            start_all(1, 0)

    a_off = 0
    acc = None
    for a_ref in a_refs:
        ka = a_ref.shape[1]
        part = jnp.dot(a_ref[...], wbf[slot, pl.ds(a_off, ka), :], preferred_element_type=F32)
        acc = part if acc is None else acc + part
        a_off += ka
    accs = [acc[:, w * bn:(w + 1) * bn] for w in range(n_w)]
    results = epilogue(accs, extra)
    for o_ref, r in zip(outs, results):
        o_ref[...] = r.astype(o_ref.dtype)

    has_next = jb + 1 < nj

    @pl.when(has_next)
    def _():
        finish_all(jb + 1, i, 1 - slot)

    @pl.when(has_next & (i + 1 < nm))
    def _():
        start_all(jb + 1, i + 1)

    @pl.when((jb + 2 < nj) & (i + 1 == nm))
    def _():
        start_all(jb + 2, 0)


def _ws_matmul(a, weights, wsrc, *, k, n, bm, bn, out_dtypes, epilogue, name,
               extras=(), a_follows_block=False):
    a_parts = a if isinstance(a, (tuple, list)) else (a,)
    m = a_parts[0].shape[0]
    nm = m // bm
    nj = n // bn
    kc = k // nm
    n_w = len(wsrc)
    n_stage = 2 if nm > 1 else 1
    assert m % bm == 0 and n % bn == 0 and k % nm == 0 and kc % (2 * V7X_SUBLANES) == 0
    if a_follows_block:
        assert len(a_parts) == 1 and a_parts[0].shape[1] == k * nj
        a_specs = [pl.BlockSpec((bm, k), lambda jb, i: (i, jb))]
    else:
        assert sum(p.shape[1] for p in a_parts) == k
        a_specs = [pl.BlockSpec((bm, p.shape[1]), lambda jb, i: (i, 0)) for p in a_parts]
    out_spec = pl.BlockSpec((bm, bn), lambda jb, i: (i, jb))
    pipelined = (_nbytes((bm, k), BF16) + sum(_nbytes((bm, bn), d) for d in out_dtypes)
                 + len(extras) * _nbytes((V7X_SUBLANES, bn), F32))
    scratch = 2 * n_w * _nbytes((k, bn), BF16) + n_stage * n_w * _nbytes((kc, bn), F32)
    temps = (n_w + 1) * _nbytes((bm, bn), F32)
    kern = functools.partial(_ws_kernel, n_a=len(a_parts), n_arr=len(weights), n_extra=len(extras),
                             n_out=len(out_dtypes), wsrc=wsrc, kc=kc, nm=nm, nj=nj, bn=bn, epilogue=epilogue)
    return pl.pallas_call(
        kern,
        out_shape=tuple(jax.ShapeDtypeStruct((m, n), d) for d in out_dtypes),
        grid=(nj, nm),
        in_specs=(a_specs + [pl.BlockSpec(memory_space=pl.ANY)] * len(weights)
                  + [pl.BlockSpec((1, bn), lambda jb, i: (0, jb))] * len(extras)),
        out_specs=tuple(out_spec for _ in out_dtypes),
        scratch_shapes=[pltpu.VMEM((2, k, n_w * bn), BF16),
                        pltpu.VMEM((n_stage, n_w, kc, bn), F32),
                        pltpu.SemaphoreType.DMA((n_stage, n_w))],
        compiler_params=pltpu.CompilerParams(
            dimension_semantics=("arbitrary", "arbitrary"),
            vmem_limit_bytes=_vmem_limit(pipelined, scratch, temps)),
        name=name,
    )(*a_parts, *weights, *extras)


def _ep_identity(accs, extra):
    return (accs[0],)


def _ep_silu_mul(accs, extra):
    return (_silu(accs[0]) * accs[1],)


def _ep_glu(accs, extra):
    return (accs[0] * _sigmoid(accs[1]),)


def _ep_pair(accs, extra):
    return (accs[0], accs[1])


def _ep_scale(accs, extra):
    return (accs[0] * extra[0][...],)


def _ep_gating_inputs(accs, extra):
    gv = extra[0][...]
    v = jax.nn.gelu(accs[1])
    heads = []
    for h in range(v.shape[1] // A_HEAD_DIM):
        cols = slice(h * A_HEAD_DIM, (h + 1) * A_HEAD_DIM)
        heads.append(_layer_norm(v[:, cols], gv[:, cols]))
    return (jax.nn.gelu(accs[0]), jnp.concatenate(heads, axis=1))


def _dense(a, w, lead, *, k, n, out_dtype, name, bm=MM_BM, bn=MM_BN):
    return _ws_matmul(a, [w], ((0, lead, 0, 0, 0, bn),), k=k, n=n, bm=bm, bn=bn,
                      out_dtypes=(out_dtype,), epilogue=_ep_identity, name=name)[0]


RES_KC = 64
RES_STAGES = 4


def _proj_residual_kernel(*refs, n_a, lead, k):
    a_refs = refs[:n_a]
    w_hbm, x_ref, gp_ref, gn_ref, xo_ref, ho_ref, wbf, y_scr, stage, sem = refs[n_a:]
    n_chunks = k // RES_KC
    i = pl.program_id(0)
    slot = lax.rem(i, 2)

    def chunk_copy(c):
        return pltpu.make_async_copy(w_hbm.at[(*lead, pl.ds(c * RES_KC, RES_KC), slice(None))],
                                     stage.at[c % RES_STAGES], sem.at[c % RES_STAGES])

    @pl.when(i == 0)
    def _():
        y_scr[1] = jnp.zeros(y_scr.shape[1:], F32)
        ahead = RES_STAGES - 1
        for c in range(min(ahead, n_chunks)):
            chunk_copy(c).start()
        for c in range(n_chunks):
            if c + ahead < n_chunks:
                chunk_copy(c + ahead).start()
            chunk_copy(c).wait()
            wbf[pl.ds(c * RES_KC, RES_KC), :] = stage[c % RES_STAGES].astype(BF16)

    def body(cur, prev):
        xn = x_ref[...] + _rms(y_scr[prev], gp_ref[...])
        xo_ref[...] = xn
        ho_ref[...] = _rms(xn, gn_ref[...]).astype(ho_ref.dtype)
        a_off = 0
        y = None
        for a_ref in a_refs:
            ka = a_ref.shape[1]
            part = jnp.dot(a_ref[...], wbf[pl.ds(a_off, ka), :], preferred_element_type=F32)
            y = part if y is None else y + part
            a_off += ka
        y_scr[cur] = y

    @pl.when(slot == 0)
    def _():
        body(0, 1)

    @pl.when(slot == 1)
    def _():
        body(1, 0)


def _proj_residual(a, w, lead, x, g_post, g_next, *, k, bm, name):
    a_parts = a if isinstance(a, (tuple, list)) else (a,)
    m, d = x.shape
    assert sum(p.shape[1] for p in a_parts) == k and k % RES_KC == 0 and m % bm == 0
    n_blocks = m // bm
    row_spec = pl.BlockSpec((bm, d), lambda i: (jnp.maximum(i - 1, 0), 0))
    g_spec = pl.BlockSpec((1, d), lambda i: (0, 0))
    pipelined = _nbytes((bm, k), BF16) + 2 * _nbytes((bm, d), F32) + _nbytes((bm, d), BF16)
    scratch = _nbytes((k, d), BF16) + RES_STAGES * _nbytes((RES_KC, d), F32) + 2 * _nbytes((bm, d), F32)
    temps = 2 * _nbytes((bm, d), F32)
    kern = functools.partial(_proj_residual_kernel, n_a=len(a_parts), lead=lead, k=k)
    return pl.pallas_call(
        kern,
        out_shape=(jax.ShapeDtypeStruct((m, d), F32), jax.ShapeDtypeStruct((m, d), BF16)),
        grid=(n_blocks + 1,),
        in_specs=([pl.BlockSpec((bm, p.shape[1]), lambda i: (jnp.minimum(i, n_blocks - 1), 0))
                   for p in a_parts]
                  + [pl.BlockSpec(memory_space=pl.ANY), row_spec, g_spec, g_spec]),
        out_specs=(row_spec, row_spec),
        scratch_shapes=[pltpu.VMEM((k, d), BF16), pltpu.VMEM((2, bm, d), F32),
                        pltpu.VMEM((RES_STAGES, RES_KC, d), F32), pltpu.SemaphoreType.DMA((RES_STAGES,))],
        compiler_params=pltpu.CompilerParams(
            dimension_semantics=("arbitrary",),
            vmem_limit_bytes=_vmem_limit(pipelined, scratch, temps)),
        name=name,
    )(*a_parts, w, x, g_post, g_next)


def _fill_shifted(zs, phases):
    rows = zs.shape[1] - V7X_SUBLANES
    for p in phases:
        if p:
            zs[p, pl.ds(0, rows), :] = zs[0, pl.ds(p, rows), :]


def _read_shifted(zs, row, n_rows):
    p = row % V7X_SUBLANES
    return zs[p, pl.ds(row - p, n_rows), :]


def _a_mixer_kernel(u_ref, v_ref, ws_ref, bs_ref, o_ref):
    t_idx = lax.broadcasted_iota(jnp.int32, (CHUNK, CHUNK), 0)
    s_idx = lax.broadcasted_iota(jnp.int32, (CHUNK, CHUNK), 1)
    causal = s_idx <= t_idx
    for h in range(A_HEADS):
        cols = pl.ds(h * A_HEAD_DIM, A_HEAD_DIM)
        w = jnp.where(causal, ws_ref[h], 0.0).astype(BF16)
        bias = bs_ref[h]
        for c in range(MIX_BM // CHUNK):
            rows = pl.ds(c * CHUNK, CHUNK)
            vm = jnp.dot(w, v_ref[rows, cols], preferred_element_type=F32) + bias
            o_ref[rows, cols] = (u_ref[rows, cols] * vm).astype(o_ref.dtype)


def _a_mixer(ug, vln, w_s, b_s):
    t = ug.shape[0]
    blocks = (_nbytes((MIX_BM, D_A), F32) + 2 * _nbytes((MIX_BM, D_A), BF16)
              + _nbytes((A_HEADS, CHUNK, CHUNK), F32) + _nbytes((A_HEADS, CHUNK, 128), F32))
    row_spec = pl.BlockSpec((MIX_BM, D_A), lambda i: (i, 0))
    return pl.pallas_call(
        _a_mixer_kernel,
        out_shape=jax.ShapeDtypeStruct((t, D_A), BF16),
        grid=(t // MIX_BM,),
        in_specs=[row_spec, row_spec,
                  pl.BlockSpec((A_HEADS, CHUNK, CHUNK), lambda i: (0, 0, 0)),
                  pl.BlockSpec((A_HEADS, CHUNK, 1), lambda i: (0, 0, 0))],
        out_specs=row_spec,
        compiler_params=pltpu.CompilerParams(
            dimension_semantics=("parallel",),
            vmem_limit_bytes=_vmem_limit(blocks)),
        name="gating_mixer",
    )(ug, vln, w_s, b_s.reshape(A_HEADS, CHUNK, 1))


_CONV_TAP0 = CONV_HALO - (CONV_WIDTH - 1)


def _b_mixer_kernel(z_ref, hz_ref, cw_ref, cb_ref, gn_ref, o_ref, zs):
    i = pl.program_id(0)
    at_seq_start = lax.rem(i * MIX_BM, SEQ) == 0
    zs[0, pl.ds(0, CONV_HALO), :] = jnp.where(at_seq_start, 0.0, hz_ref[...])
    zs[0, pl.ds(CONV_HALO, MIX_BM), :] = z_ref[...]
    _fill_shifted(zs, range(V7X_SUBLANES))
    cb = cb_ref[...]
    gn = gn_ref[...]
    for r in range(MIX_BM // ROW_CHUNK):
        r0 = r * ROW_CHUNK
        acc = jnp.zeros((ROW_CHUNK, B_GROUP_DIM), F32)
        for k in range(CONV_WIDTH):
            acc = acc + cw_ref[pl.ds(k, 1), :] * _read_shifted(zs, r0 + _CONV_TAP0 + k, ROW_CHUNK)
        y = _layer_norm(acc + cb, gn)
        o_ref[pl.ds(r0, ROW_CHUNK), :] = _silu(y).astype(o_ref.dtype)


def _b_mixer(z, conv_w, conv_b, gn_g):
    t = z.shape[0]
    gd = B_GROUP_DIM
    halo_per_blk = MIX_BM // CONV_HALO
    blocks = (_nbytes((MIX_BM, gd), F32) + 2 * _nbytes((CONV_HALO, gd), F32) + _nbytes((MIX_BM, gd), BF16))
    scratch = _nbytes((V7X_SUBLANES, MIX_BM + CONV_HALO, gd), F32)
    return pl.pallas_call(
        _b_mixer_kernel,
        out_shape=jax.ShapeDtypeStruct((t, D_B), BF16),
        grid=(t // MIX_BM, B_GROUPS),
        in_specs=[pl.BlockSpec((MIX_BM, gd), lambda i, g: (i, g)),
                  pl.BlockSpec((CONV_HALO, gd), lambda i, g: (jnp.maximum(i * halo_per_blk - 1, 0), g)),
                  pl.BlockSpec((CONV_WIDTH, gd), lambda i, g: (0, g)),
                  pl.BlockSpec((1, gd), lambda i, g: (0, g)),
                  pl.BlockSpec((1, gd), lambda i, g: (0, g))],
        out_specs=pl.BlockSpec((MIX_BM, gd), lambda i, g: (i, g)),
        scratch_shapes=[pltpu.VMEM((V7X_SUBLANES, MIX_BM + CONV_HALO, gd), F32)],
        compiler_params=pltpu.CompilerParams(
            dimension_semantics=("parallel", "parallel"),
            vmem_limit_bytes=_vmem_limit(blocks, scratch)),
        name="conv_mixer",
    )(z, z, conv_w, conv_b.reshape(1, D_B), gn_g.reshape(1, D_B))


def _pool_kernel(p_ref, h_ref, o_ref, zs):
    i = pl.program_id(0)
    g = pl.program_id(1)
    t0 = lax.rem(i * MIX_BM, SEQ)
    zs[0, pl.ds(0, POOL_HALO), :] = jnp.where(t0 == 0, 0.0, h_ref[...])
    zs[0, pl.ds(POOL_HALO, MIX_BM), :] = p_ref[...]

    for gi, win in enumerate(POOL_WINDOWS):

        @pl.when(g == gi)
        def _(win=win):
            _fill_shifted(zs, sorted({(POOL_HALO - j) % V7X_SUBLANES for j in range(win)}))
            for r in range(MIX_BM // ROW_CHUNK):
                r0 = r * ROW_CHUNK
                cur = zs[0, pl.ds(POOL_HALO + r0, ROW_CHUNK), :]
                s = cur
                for j in range(1, win):
                    s = s + _read_shifted(zs, POOL_HALO + r0 - j, ROW_CHUNK)
                t = t0 + r0 + lax.broadcasted_iota(jnp.int32, (ROW_CHUNK, 1), 0)
                count = jnp.minimum(t + 1, win).astype(F32)
                o_ref[pl.ds(r0, ROW_CHUNK), :] = (s / count - cur).astype(o_ref.dtype)


def _pool(p):
    t, d = p.shape
    cols_per_group = C_GROUP_DIM // POOL_BC
    halo_per_blk = MIX_BM // POOL_HALO
    blocks = (_nbytes((MIX_BM, POOL_BC), F32) + _nbytes((POOL_HALO, POOL_BC), F32)
              + _nbytes((MIX_BM, POOL_BC), BF16))
    scratch = _nbytes((V7X_SUBLANES, MIX_BM + POOL_HALO, POOL_BC), F32)
    return pl.pallas_call(
        _pool_kernel,
        out_shape=jax.ShapeDtypeStruct((t, d), BF16),
        grid=(t // MIX_BM, C_GROUPS, cols_per_group),
        in_specs=[pl.BlockSpec((MIX_BM, POOL_BC), lambda i, g, c: (i, g * cols_per_group + c)),
                  pl.BlockSpec((POOL_HALO, POOL_BC),
                               lambda i, g, c: (jnp.maximum(i * halo_per_blk - 1, 0),
                                                g * cols_per_group + c))],
        out_specs=pl.BlockSpec((MIX_BM, POOL_BC), lambda i, g, c: (i, g * cols_per_group + c)),
        scratch_shapes=[pltpu.VMEM((V7X_SUBLANES, MIX_BM + POOL_HALO, POOL_BC), F32)],
        compiler_params=pltpu.CompilerParams(
            dimension_semantics=("parallel", "parallel", "parallel"),
            vmem_limit_bytes=_vmem_limit(blocks, scratch)),
        name="pool_mixer",
    )(p, p)


def _attn_kernel(q_ref, k_ref, v_ref, o_ref):
    for h in range(X_HEADS):
        cols = pl.ds(h * X_HEAD_DIM, X_HEAD_DIM)
        s = lax.dot_general(q_ref[:, cols], k_ref[:, cols], (((1,), (1,)), ((), ())),
                            preferred_element_type=F32) * (X_HEAD_DIM ** -0.5)
        e = jnp.exp(s - jnp.max(s, axis=-1, keepdims=True))
        p = e / jnp.sum(e, axis=-1, keepdims=True)
        o_ref[:, cols] = jnp.dot(p.astype(BF16), v_ref[:, cols],
                                 preferred_element_type=F32).astype(o_ref.dtype)


def _attention(q, k, v):
    t = q.shape[0]
    q_blocks_per_batch = SEQ // ATT_BQ
    blocks = 2 * _nbytes((ATT_BQ, X_DIM), BF16) + 2 * _nbytes((MEM_LEN, X_DIM), BF16)
    q_spec = pl.BlockSpec((ATT_BQ, X_DIM), lambda i: (i, 0))
    kv_spec = pl.BlockSpec((MEM_LEN, X_DIM), lambda i: (i // q_blocks_per_batch, 0))
    return pl.pallas_call(
        _attn_kernel,
        out_shape=jax.ShapeDtypeStruct((t, X_DIM), BF16),
        grid=(t // ATT_BQ,),
        in_specs=[q_spec, kv_spec, kv_spec],
        out_specs=q_spec,
        compiler_params=pltpu.CompilerParams(
            dimension_semantics=("parallel",),
            vmem_limit_bytes=_vmem_limit(blocks, 0, 4 * X_HEADS * _nbytes((ATT_BQ, MEM_LEN), F32))),
        name="mem_attention",
    )(q, k, v)


def kernel(x, mem, mix_pre_g, mix_post_g, xattn_pre_g, mem_norm_g, xattn_post_g, ffn_pre_g, ffn_post_g, ab_w_in, a_w_s, a_b_s, a_v_norm_g, b_conv_w, b_conv_b, b_gn_g, ab_w_out, c_w_in, c_group_w, c_scale, c_w_out, xq_w, xk_w, xv_w, xo_w, ffn_gate_w, ffn_up_w, ffn_down_w):
    def gain(g, i):
        return g[i].reshape(1, D_MODEL)

    xf = x.reshape(TOKENS, D_MODEL)
    memf = mem.reshape(BATCH * MEM_LEN, D_MODEL)
    group_w = c_group_w.reshape(DEPTH // 2, D_MODEL, C_GROUP_DIM)
    h = _rmsnorm(xf, gain(mix_pre_g, 0), "pre_norm")

    for i in range(DEPTH):
        j = i // 2
        if i % 2 == 0:
            bn = PAIR_BN
            ug, vln = _ws_matmul(
                h, [ab_w_in], ((0, (j,), 0, 0, 0, bn), (0, (j,), 0, D_A, 0, bn)),
                k=D_MODEL, n=D_A, bm=MM_BM, bn=bn, out_dtypes=(F32, BF16), epilogue=_ep_gating_inputs,
                extras=(a_v_norm_g[j].reshape(1, D_A),), name="mix_in_gating")
            (zb,) = _ws_matmul(
                h, [ab_w_in], ((0, (j,), 0, 2 * D_A, 0, bn), (0, (j,), 0, 2 * D_A + D_B, 0, bn)),
                k=D_MODEL, n=D_B, bm=MM_BM, bn=bn, out_dtypes=(F32,), epilogue=_ep_glu, name="mix_in_glu")
            ya = _a_mixer(ug, vln, a_w_s[j], a_b_s[j])
            yb = _b_mixer(zb, b_conv_w[j], b_conv_b[j], b_gn_g[j])
            xf, h = _proj_residual((ya, yb), ab_w_out, (j,), xf, gain(mix_post_g, i), gain(xattn_pre_g, i),
                                   k=D_A + D_B, bm=RES_BM_WIDE, name="mix_out_residual")
        else:
            p = _pool(_dense(h, c_w_in, (j,), k=D_MODEL, n=D_MODEL, out_dtype=F32, name="pool_in"))
            (yg,) = _ws_matmul(
                p, [group_w], ((0, (j,), 0, 0, C_GROUP_DIM, 0),),
                k=C_GROUP_DIM, n=D_MODEL, bm=MM_BM, bn=C_GROUP_DIM, out_dtypes=(BF16,), epilogue=_ep_scale,
                extras=(c_scale[j].reshape(1, D_MODEL),), a_follows_block=True, name="pool_group")
            xf, h = _proj_residual(yg, c_w_out, (j,), xf, gain(mix_post_g, i), gain(xattn_pre_g, i),
                                   k=D_MODEL, bm=RES_BM_WIDE, name="pool_out_residual")

        m = _rmsnorm(memf, gain(mem_norm_g, i), "mem_norm")
        q = _dense(h, xq_w, (i,), k=D_MODEL, n=X_DIM, out_dtype=BF16, name="attn_q")
        kk, vv = _ws_matmul(
            m, [xk_w, xv_w], ((0, (i,), 0, 0, 0, PAIR_BN), (1, (i,), 0, 0, 0, PAIR_BN)),
            k=D_MODEL, n=X_DIM, bm=BATCH * MEM_LEN, bn=PAIR_BN, out_dtypes=(BF16, BF16), epilogue=_ep_pair,
            name="attn_kv")
        o = _attention(q, kk, vv)
        xf, h = _proj_residual(o, xo_w, (i,), xf, gain(xattn_post_g, i), gain(ffn_pre_g, i),
                               k=X_DIM, bm=RES_BM, name="attn_out_residual")

        (hid,) = _ws_matmul(
            h, [ffn_gate_w, ffn_up_w], ((0, (i,), 0, 0, 0, FF_BN), (1, (i,), 0, 0, 0, FF_BN)),
            k=D_MODEL, n=D_FF, bm=FF_BM, bn=FF_BN, out_dtypes=(BF16,), epilogue=_ep_silu_mul, name="ffn_gate_up")
        y = _dense(hid, ffn_down_w, (i,), k=D_FF, n=D_MODEL, out_dtype=F32, name="ffn_down",
                   bm=DOWN_BM, bn=DOWN_BN)
        g_next = gain(mix_pre_g, i + 1) if i + 1 < DEPTH else None
        xf, h = _residual(xf, y, gain(ffn_post_g, i), g_next)

    return xf.reshape(BATCH, SEQ, D_MODEL)
```
